```python
import math
import jax, jax.numpy as jnp
from jax import lax
import numpy as np

D_MODEL = 2048
BATCH = 4
SEQ = 2048
DEPTH = 4
DEC_BATCH = 8
DEC_SEQ = 8
PAST_LEN = 16384
PAGE_SIZE = 128

GLA_HEADS = 4
GLA_DK = D_MODEL // 16
GLA_DV = D_MODEL // 8
GLA_RANK = 16
GLA_TAU = 16.0
GLA_CHUNK = 64
ATT_HEADS = 8
ATT_DH = D_MODEL // 16
IDX_HEADS = 16
IDX_DIM = D_MODEL // 16
TOPK_MAX = 256
Q_BLOCK = 128
EPS = 1e-6

GLA_QK_W = GLA_HEADS * GLA_DK
GLA_V_W = GLA_HEADS * GLA_DV
ATT_W = ATT_HEADS * ATT_DH
IDX_Q_W = IDX_HEADS * IDX_DIM
IDX_SCALE = (IDX_DIM * IDX_HEADS) ** -0.5
IN_SIZES = (GLA_QK_W, GLA_QK_W, GLA_V_W, GLA_V_W, GLA_RANK,
            ATT_W, ATT_W, ATT_W, ATT_W, IDX_Q_W, IDX_DIM, IDX_HEADS,
            D_MODEL, D_MODEL)
IN_W = sum(IN_SIZES)
SPLIT_POINTS = tuple(int(s) for s in np.cumsum(IN_SIZES)[:-1])

kernel_name = "hybrid_gla_dsa_alibi_adaln_step"


def rmsnorm(x, g):
    xf = x.astype(jnp.float32)
    y = xf * lax.rsqrt(jnp.mean(xf * xf, axis=-1, keepdims=True) + EPS)
    return (y * g.astype(jnp.float32)).astype(x.dtype)


def alibi_slopes(n):
    return 2.0 ** (-8.0 * jnp.arange(1, n + 1, dtype=jnp.float32) / n)


def gla_chunked(q, k, v, log_a, s0):
    B, T, H, DK = q.shape
    DV = v.shape[-1]
    C = math.gcd(T, GLA_CHUNK)
    n = T // C
    f32 = jnp.float32

    def chunks(a):
        return jnp.moveaxis(a.astype(f32).reshape((B, n, C) + a.shape[2:]), 1, 0)

    causal = jnp.tril(jnp.ones((C, C), dtype=bool))[None, :, :, None, None]

    def step(S, inp):
        qc, kc, vc, gc = inp
        b = jnp.cumsum(gc, axis=1)
        diff = b[:, :, None] - b[:, None, :]
        decay = jnp.exp(jnp.where(causal, diff, -jnp.inf))
        A = jnp.einsum('bthd,bshd,btshd->bhts', qc, kc, decay)
        o = jnp.einsum('bhts,bshv->bthv', A, vc)
        o = o + jnp.einsum('bthk,bhkv->bthv', qc * jnp.exp(b), S)
        b_last = b[:, -1]
        k_dec = kc * jnp.exp(b_last[:, None] - b)
        S_new = jnp.exp(b_last)[..., None] * S + jnp.einsum('bshk,bshv->bhkv', k_dec, vc)
        return S_new, o

    S, o = lax.scan(step, s0.astype(f32), (chunks(q), chunks(k), chunks(v), chunks(log_a)))
    o = jnp.moveaxis(o, 0, 1).reshape(B, T, H, DV)
    return o.astype(v.dtype), S


def indexer_scores(iq, iw, ik):
    s = jnp.einsum('bqhd,bsd->bqhs', iq.astype(jnp.float32), ik.astype(jnp.float32))
    return jnp.einsum('bqhs,bqh->bqs', jax.nn.relu(s), iw.astype(jnp.float32) * IDX_SCALE)


def sparse_attend(q, k_sel, v_sel, pos, idx):
    dist = pos[:, None] - idx
    logits = jnp.einsum('bqhd,bqkhd->bhqk', q.astype(jnp.float32),
                        k_sel.astype(jnp.float32)) * (ATT_DH ** -0.5)
    logits = logits - alibi_slopes(q.shape[2])[None, :, None, None] * dist[:, None].astype(jnp.float32)
    logits = jnp.where((dist >= 0)[:, None], logits, -jnp.inf)
    p = jax.nn.softmax(logits, axis=-1)
    out = jnp.einsum('bhqk,bqkhd->bqhd', p, v_sel.astype(jnp.float32))
    return out.astype(q.dtype)


def prompt_sparse_attention(q, k, v, iq, ik, iw):
    B, T = q.shape[:2]
    topk = min(TOPK_MAX, T // 4)
    qb = math.gcd(T, Q_BLOCK)
    nb = T // qb
    key_pos = jnp.arange(T)

    def blocks(a):
        return jnp.moveaxis(a.reshape((B, nb, qb) + a.shape[2:]), 1, 0)

    def one_block(args):
        q_b, iq_b, iw_b, start = args
        pos = start + jnp.arange(qb)
        scores = indexer_scores(iq_b, iw_b, ik)
        scores = jnp.where(key_pos[None, None, :] <= pos[None, :, None], scores, -jnp.inf)
        _, idx = lax.top_k(scores, topk)
        k_sel = jax.vmap(lambda kk, ii: kk[ii])(k, idx)
        v_sel = jax.vmap(lambda vv, ii: vv[ii])(v, idx)
        return sparse_attend(q_b, k_sel, v_sel, pos, idx)

    starts = jnp.arange(nb, dtype=jnp.int32) * qb
    out = lax.map(one_block, (blocks(q), blocks(iq), blocks(iw), starts))
    return jnp.moveaxis(out, 0, 1).reshape(q.shape)


def make_sample_attention(cache_k, cache_v, cache_kidx, page_table, layer):
    def attend(q, k, v, iq, ik, iw):
        B, T = q.shape[:2]
        n_pages = page_table.shape[1]
        past = n_pages * PAGE_SIZE
        L = past + T
        topk = min(TOPK_MAX, L // 4)
        ik_past = cache_kidx[layer, page_table].reshape(B, past, ik.shape[-1])
        ik_all = jnp.concatenate([ik_past.astype(ik.dtype), ik], axis=1)
        pos = past + jnp.arange(T)
        scores = indexer_scores(iq, iw, ik_all)
        scores = jnp.where(jnp.arange(L)[None, None, :] <= pos[None, :, None], scores, -jnp.inf)
        _, idx = lax.top_k(scores, topk)
        in_past = idx < past
        pidx = jnp.minimum(idx, past - 1)
        phys = jnp.take_along_axis(page_table, (pidx // PAGE_SIZE).reshape(B, -1), axis=1).reshape(idx.shape)
        off = pidx % PAGE_SIZE
        nidx = jnp.clip(idx - past, 0, T - 1)

        def gather(pool, new):
            from_pool = pool[layer, phys, off].astype(new.dtype)
            from_new = jax.vmap(lambda a, i: a[i])(new, nidx)
            return jnp.where(in_past[..., None, None], from_pool, from_new)

        return sparse_attend(q, gather(cache_k, k), gather(cache_v, v), pos, idx)
    return attend


def mixer_layer(x, c, s0, attend, w_c, b_c, g_norm, w_in, w_a2, b_a2, g_gla, w_pa, w_pb, w_out):
    B, T, _ = x.shape
    shift, scale, gate = jnp.split(c @ w_c + b_c, 3, axis=-1)
    h = rmsnorm(x, g_norm) * (1.0 + scale[:, None]) + shift[:, None]
    (gq, gk, gv, gz, ga, aq, ak, av, az, iq, ik, iw, m_a, m_b) = jnp.split(h @ w_in, SPLIT_POINTS, axis=-1)

    def heads(a, n):
        return a.reshape(B, T, n, -1)

    log_a = jax.nn.log_sigmoid((ga @ w_a2 + b_a2).astype(jnp.float32)) / GLA_TAU
    o_a, s_new = gla_chunked(heads(gq, GLA_HEADS) * (GLA_DK ** -0.5), heads(gk, GLA_HEADS),
                             heads(gv, GLA_HEADS), heads(log_a, GLA_HEADS), s0)
    o_a = rmsnorm(o_a, g_gla).reshape(B, T, GLA_V_W) * jax.nn.silu(gz)

    k = heads(ak, ATT_HEADS)
    v = heads(av, ATT_HEADS)
    o_b = attend(heads(aq, ATT_HEADS), k, v, heads(iq, IDX_HEADS), ik, iw)
    o_b = o_b.reshape(B, T, ATT_W) * jax.nn.silu(az)

    merged = jax.nn.sigmoid(m_a) * (o_a @ w_pa) + jax.nn.sigmoid(m_b) * (o_b @ w_pb)
    x = x + gate[:, None] * (merged @ w_out)
    return x, s_new.astype(s0.dtype), k, v, ik


def setup_inputs(seed: int = 0) -> dict:
    key = jax.random.key(seed)
    ks = jax.random.split(key, 24)
    f32 = jnp.float32
    n_pages = PAST_LEN // PAGE_SIZE
    used = DEC_BATCH * n_pages
    n_pool = used + max(1, used // 4)
    nrm = lambda k, s: jax.random.normal(k, s, f32)
    page_table = jax.random.permutation(ks[0], n_pool)[:used].reshape(DEC_BATCH, n_pages).astype(jnp.int32)
    return {
        "x_prompt": nrm(ks[1], (BATCH, SEQ, D_MODEL)),
        "x_sample": nrm(ks[2], (DEC_BATCH, DEC_SEQ, D_MODEL)),
        "cache_k": nrm(ks[3], (DEPTH, n_pool, PAGE_SIZE, ATT_HEADS, ATT_DH)),
        "cache_v": nrm(ks[4], (DEPTH, n_pool, PAGE_SIZE, ATT_HEADS, ATT_DH)),
        "cache_kidx": nrm(ks[5], (DEPTH, n_pool, PAGE_SIZE, IDX_DIM)),
        "state_gla": nrm(ks[6], (DEPTH, DEC_BATCH, GLA_HEADS, GLA_DK, GLA_DV)),
        "page_table": page_table,
        "c_prompt": nrm(ks[7], (BATCH, D_MODEL)),
        "c_sample": nrm(ks[8], (DEC_BATCH, D_MODEL)),
        "w_c": nrm(ks[9], (DEPTH, D_MODEL, 3 * D_MODEL)) * (0.3 * D_MODEL ** -0.5),
        "b_c": nrm(ks[10], (DEPTH, 3 * D_MODEL)) * 0.02,
        "g_norm": 1.0 + 0.02 * nrm(ks[11], (DEPTH, D_MODEL)),
        "w_in": nrm(ks[12], (DEPTH, D_MODEL, IN_W)) * D_MODEL ** -0.5,
        "w_a2": nrm(ks[13], (DEPTH, GLA_RANK, GLA_QK_W)) * GLA_RANK ** -0.5,
        "b_a2": nrm(ks[14], (DEPTH, GLA_QK_W)) * 0.1,
        "g_gla": 1.0 + 0.02 * nrm(ks[15], (DEPTH, GLA_HEADS, GLA_DV)),
        "w_pa": nrm(ks[16], (DEPTH, GLA_V_W, D_MODEL)) * GLA_V_W ** -0.5,
        "w_pb": nrm(ks[17], (DEPTH, ATT_W, D_MODEL)) * ATT_W ** -0.5,
        "w_out": nrm(ks[18], (DEPTH, D_MODEL, D_MODEL)) * D_MODEL ** -0.5,
        "g_final": 1.0 + 0.02 * nrm(ks[19], (D_MODEL,)),
    }


def reference(x_prompt, x_sample, cache_k, cache_v, cache_kidx, state_gla, page_table,
              c_prompt, c_sample, w_c, b_c, g_norm, w_in, w_a2, b_a2, g_gla, w_pa, w_pb, w_out, g_final):
    xp, xs = x_prompt, x_sample
    kp_l, vp_l, ikp_l, sp_l = [], [], [], []
    ks_l, vs_l, iks_l, ss_l = [], [], [], []
    for l in range(DEPTH):
        lw = (w_c[l], b_c[l], g_norm[l], w_in[l], w_a2[l], b_a2[l], g_gla[l], w_pa[l], w_pb[l], w_out[l])
        s0p = jnp.zeros((xp.shape[0], GLA_HEADS, GLA_DK, GLA_DV), state_gla.dtype)
        xp, sp, kp, vp, ikp = mixer_layer(xp, c_prompt, s0p, prompt_sparse_attention, *lw)
        attend_s = make_sample_attention(cache_k, cache_v, cache_kidx, page_table, l)
        xs, ss, ks_, vs, iks = mixer_layer(xs, c_sample, state_gla[l], attend_s, *lw)
        kp_l.append(kp); vp_l.append(vp); ikp_l.append(ikp); sp_l.append(sp)
        ks_l.append(ks_); vs_l.append(vs); iks_l.append(iks); ss_l.append(ss)
    y_prompt = rmsnorm(xp, g_final)
    y_sample = rmsnorm(xs, g_final)
    return (y_prompt, y_sample,
            jnp.stack(kp_l), jnp.stack(vp_l), jnp.stack(ikp_l), jnp.stack(sp_l),
            jnp.stack(ks_l), jnp.stack(vs_l), jnp.stack(iks_l), jnp.stack(ss_l))
```

```python
import functools

import numpy as np
import jax
import jax.numpy as jnp
from jax import lax
from jax.experimental import pallas as pl
from jax.experimental.pallas import tpu as pltpu

F32 = jnp.float32
BF16 = jnp.bfloat16
I32 = jnp.int32

EPS = 1e-6
PAGE_SIZE = 128
GLA_HEADS = 4
GLA_RANK = 16
GLA_TAU = 16.0
GLA_CHUNK = 64
ATT_HEADS = 8
IDX_HEADS = 16
TOPK_MAX = 256

LANES = 128
SUBLANES = 8
VMEM_LIMIT_BYTES = 56 * 1024 * 1024

NEG_BIG = -1e30
INT_MIN = -2147483648
KEY_NEG_INF = -2139095041


def _cparams(sem):
    return pltpu.CompilerParams(dimension_semantics=sem, vmem_limit_bytes=VMEM_LIMIT_BYTES)


def _dot(a, b):
    return jnp.dot(a, b, preferred_element_type=F32)


def _dot_nt(a, b):
    return lax.dot_general(a, b, (((1,), (1,)), ((), ())), preferred_element_type=F32)


def _sigmoid(x):
    return 1.0 / (1.0 + jnp.exp(-x))


def _sort_key(x):
    bits = pltpu.bitcast(x + 0.0, I32)
    return jnp.where(bits >= 0, bits, bits ^ jnp.int32(0x7FFFFFFF))


def _mod_kernel(c_ref, w_ref, b_ref, o_ref):
    o_ref[0] = _dot(c_ref[...].astype(BF16), w_ref[0].astype(BF16)) + b_ref[0]


def _modulation(c_all, w_c, b_c):
    depth, d, n = w_c.shape
    rows = c_all.shape[0]
    tn = 768
    return pl.pallas_call(
        _mod_kernel,
        grid=(depth, n // tn),
        in_specs=[pl.BlockSpec((rows, d), lambda l, j: (0, 0)),
                  pl.BlockSpec((1, d, tn), lambda l, j: (l, 0, j)),
                  pl.BlockSpec((1, 1, tn), lambda l, j: (l, 0, j))],
        out_specs=pl.BlockSpec((1, rows, tn), lambda l, j: (l, 0, j)),
        out_shape=jax.ShapeDtypeStruct((depth, rows, n), F32),
        compiler_params=_cparams(("parallel", "parallel")),
        name="adaln_modulation",
    )(c_all, w_c, b_c.reshape(depth, 1, n))


def _inproj_kernel(x_ref, sc_ref, sh_ref, g_ref, w_ref, *rest):
    out_refs, h_ref = rest[:-1], rest[-1]

    @pl.when(pl.program_id(2) == 0)
    def _():
        x = x_ref[0]
        y = x * lax.rsqrt(jnp.mean(x * x, axis=-1, keepdims=True) + EPS) * g_ref[...]
        h_ref[...] = (y * (1.0 + sc_ref[0]) + sh_ref[0]).astype(BF16)

    r = _dot(h_ref[...], w_ref[...])
    for o in out_refs:
        o[0] = r.astype(o.dtype)


def _inproj(x, scale, shift, g, w, out_dtypes, tm, tn, name):
    B, T, D = x.shape
    N = w.shape[1]
    R = scale.shape[1]
    rb = 1 if R == 1 else tm
    mod_map = (lambda b, i, j: (b, 0, 0)) if R == 1 else (lambda b, i, j: (b, i, 0))
    return pl.pallas_call(
        _inproj_kernel,
        grid=(B, T // tm, N // tn),
        in_specs=[pl.BlockSpec((1, tm, D), lambda b, i, j: (b, i, 0)),
                  pl.BlockSpec((1, rb, D), mod_map),
                  pl.BlockSpec((1, rb, D), mod_map),
                  pl.BlockSpec((1, D), lambda b, i, j: (0, 0)),
                  pl.BlockSpec((D, tn), lambda b, i, j: (0, j))],
        out_specs=[pl.BlockSpec((1, tm, tn), lambda b, i, j: (b, i, j)) for _ in out_dtypes],
        out_shape=[jax.ShapeDtypeStruct((B, T, N), dt) for dt in out_dtypes],
        scratch_shapes=[pltpu.VMEM((tm, D), BF16)],
        compiler_params=_cparams(("parallel", "parallel", "arbitrary")),
        name=name,
    )(x, scale, shift, g.reshape(1, D), w)


def _gla_constants(C):
    r = np.arange(C)[:, None]
    j = np.arange(C)[None, :]
    blocks = [(j <= r), (j > r)]
    masks = []
    w = C // 2
    while w >= 1:
        mid = (r // (2 * w)) * (2 * w) + w
        upper = r >= mid
        blocks.append(np.where(upper, (j >= mid) & (j <= r), (j > r) & (j < mid)))
        t, s = r, j
        same = (t // (2 * w)) == (s // (2 * w))
        t_up = t >= (t // (2 * w)) * (2 * w) + w
        s_lo = s < (s // (2 * w)) * (2 * w) + w
        masks.append(same & t_up & s_lo)
        w //= 2
    masks.append(r == j)
    W = np.concatenate(blocks, axis=0).astype(np.float32)
    M = np.stack(masks).astype(np.float32)
    return jnp.asarray(W, dtype=BF16), jnp.asarray(M, dtype=F32)


def _gla_kernel(q_ref, k_ref, v_ref, z_ref, ga_ref, wa_ref, ba_ref, gg_ref, s0_ref, W_ref, M_ref,
                o_ref, sout_ref, S_ref, *, C, nchunk, nlev, valid, dk):
    ci = pl.program_id(2)

    @pl.when(ci == 0)
    def _():
        S_ref[...] = s0_ref[0, 0]

    W = W_ref[...]
    S = S_ref[...]
    for c in range(nchunk):
        rows = pl.ds(c * C, C)
        qs = q_ref[0, rows, :] * (dk ** -0.5)
        k = k_ref[0, rows, :]
        v = v_ref[0, rows, :].astype(BF16)
        zz = _dot(ga_ref[0, rows, :].astype(BF16), wa_ref[...]) + ba_ref[...]
        g = -(jnp.maximum(-zz, 0.0) + jnp.log1p(jnp.exp(-jnp.abs(zz)))) * (1.0 / GLA_TAU)
        if valid is not None:
            tok = ci * (nchunk * C) + c * C + lax.broadcasted_iota(I32, g.shape, 0)
            g = jnp.where(tok < valid, g, 0.0)
        g1 = g.astype(BF16)
        r1 = g - g1.astype(F32)
        g2 = r1.astype(BF16)
        g3 = (r1 - g2.astype(F32)).astype(BF16)
        E = jnp.exp(_dot(W, g1) + _dot(W, g2) + _dot(W, g3))
        e_b = E[0:C]
        e_rest = E[C:2 * C]
        A = jnp.where(M_ref[nlev] > 0, _dot_nt(qs.astype(BF16), k.astype(BF16)), 0.0)
        for l in range(nlev):
            e_l = E[(2 + l) * C:(3 + l) * C]
            A = A + jnp.where(M_ref[l] > 0, _dot_nt((qs * e_l).astype(BF16), (k * e_l).astype(BF16)), 0.0)
        o = _dot(A.astype(BF16), v) + _dot((qs * e_b).astype(BF16), S.astype(BF16))
        k_dec = k * e_rest
        e_last = e_b.T[:, C - 1:C]
        S = e_last * S + _dot(k_dec.T.astype(BF16), v)
        y = o * lax.rsqrt(jnp.mean(o * o, axis=-1, keepdims=True) + EPS) * gg_ref[0]
        zg = z_ref[0, rows, :]
        o_ref[0, rows, :] = (y * (zg * _sigmoid(zg))).astype(o_ref.dtype)
    S_ref[...] = S

    @pl.when(ci == pl.num_programs(2) - 1)
    def _():
        sout_ref[0, 0] = S


def _gla(p1, p2, col, s0, wa, ba, gg, consts, tc, valid):
    B, T, _ = p1.shape
    H, dk, dv = s0.shape[1], s0.shape[2], s0.shape[3]
    W, M = consts
    C = GLA_CHUNK
    nlev = M.shape[0] - 1
    qo, ko, vo, zo = (col[n] // w for n, w in (("gq", dk), ("gk", dk), ("gv", dv), ("gz", dv)))
    gao = col["gaiw"] // LANES
    kern = functools.partial(_gla_kernel, C=C, nchunk=tc // C, nlev=nlev, valid=valid, dk=dk)
    return pl.pallas_call(
        kern,
        grid=(B, H, T // tc),
        in_specs=[pl.BlockSpec((1, tc, dk), lambda b, h, i: (b, i, qo + h)),
                  pl.BlockSpec((1, tc, dk), lambda b, h, i: (b, i, ko + h)),
                  pl.BlockSpec((1, tc, dv), lambda b, h, i: (b, i, vo + h)),
                  pl.BlockSpec((1, tc, dv), lambda b, h, i: (b, i, zo + h)),
                  pl.BlockSpec((1, tc, LANES), lambda b, h, i: (b, i, gao)),
                  pl.BlockSpec((LANES, dk), lambda b, h, i: (0, h)),
                  pl.BlockSpec((1, dk), lambda b, h, i: (0, h)),
                  pl.BlockSpec((1, 1, dv), lambda b, h, i: (h, 0, 0)),
                  pl.BlockSpec((1, 1, dk, dv), lambda b, h, i: (b, h, 0, 0)),
                  pl.BlockSpec(W.shape, lambda b, h, i: (0, 0)),
                  pl.BlockSpec(M.shape, lambda b, h, i: (0, 0, 0))],
        out_specs=[pl.BlockSpec((1, tc, dv), lambda b, h, i: (b, i, h)),
                   pl.BlockSpec((1, 1, dk, dv), lambda b, h, i: (b, h, 0, 0))],
        out_shape=[jax.ShapeDtypeStruct((B, T, H * dv), BF16),
                   jax.ShapeDtypeStruct((B, H, dk, dv), F32)],
        scratch_shapes=[pltpu.VMEM((dk, dv), F32)],
        compiler_params=_cparams(("parallel", "parallel", "arbitrary")),
        name="gla_branch",
    )(p1, p1, p1, p1, p2, wa, ba, gg.reshape(H, 1, dv), s0, W, M)


def _kth_key(count_ge, kk, rows):
    def body(i, u):
        cand = u | (jnp.int32(1) << (31 - i))
        ok = count_ge(cand ^ jnp.int32(INT_MIN)) >= kk
        return jnp.where(ok, cand, u)
    u = lax.fori_loop(0, 32, body, jnp.zeros((rows, 1), I32))
    return u ^ jnp.int32(INT_MIN)


def _tie_cutoff(count_eq_below, need, rows, nbits):
    def body(i, c):
        cand = c | (jnp.int32(1) << (nbits - 1 - i))
        ok = count_eq_below(cand) <= need
        return jnp.where(ok, cand, c)
    return lax.fori_loop(0, nbits, body, jnp.zeros((rows, 1), I32))


def _count(mask):
    return jnp.sum(jnp.where(mask, 1.0, 0.0), axis=1, keepdims=True)


def _dsa_kernel(iq_ref, ik_ref, iw_ref, q_ref, k_ref, v_ref, z_ref, o_ref, cut_ref, *, tq, S, topk, iw_off, scale,
                idx_scale):
    qi = pl.program_id(1)
    pos = qi * tq + lax.broadcasted_iota(I32, (tq, S), 0)
    kpos = lax.broadcasted_iota(I32, (tq, S), 1)
    causal = kpos <= pos

    ik = ik_ref[0]
    iw = iw_ref[0]
    score = jnp.zeros((tq, S), F32)
    for h in range(IDX_HEADS):
        s = _dot_nt(iq_ref[0, :, h * LANES:(h + 1) * LANES], ik)
        score = score + jnp.maximum(s, 0.0) * (iw[:, iw_off + h:iw_off + h + 1] * idx_scale)
    key = _sort_key(jnp.where(causal, score, -jnp.inf))

    kk = float(topk)
    thr = _kth_key(lambda t: _count(key >= t), kk, tq)
    n_ge = _count(key >= thr)
    cut_ref[...] = jnp.full(cut_ref.shape, S, I32)
    tied = jnp.where((n_ge > kk) & (thr > KEY_NEG_INF), 1.0, 0.0)

    @pl.when(jnp.max(tied) > 0.0)
    def _():
        need = kk - _count(key > thr)
        eq = key == thr
        c = _tie_cutoff(lambda c_: _count(eq & (kpos < c_)), need, tq, int(S).bit_length())
        cut_ref[...] = jnp.broadcast_to(c, cut_ref.shape)

    cut = cut_ref[:, 0:1]
    sel = (key > thr) | ((key == thr) & (kpos < cut))
    bias = jnp.where(sel & causal, 0.0, NEG_BIG)
    dist = (pos - kpos).astype(F32)

    for h in range(ATT_HEADS):
        cols = slice(h * LANES, (h + 1) * LANES)
        logits = _dot_nt(q_ref[0, :, cols], k_ref[0, :, cols]) * scale - (2.0 ** -(h + 1)) * dist + bias
        m = jnp.max(logits, axis=1, keepdims=True)
        p = jnp.exp(logits - m)
        l = jnp.sum(p, axis=1, keepdims=True)
        out = _dot(p.astype(BF16), v_ref[0, :, cols]) / l
        zg = z_ref[0, :, cols]
        o_ref[0, :, cols] = (out * (zg * _sigmoid(zg))).astype(o_ref.dtype)


def _dsa_prompt(p1, p2f, p2b, p3, col1, col2, col3, topk, tq):
    B, T, _ = p1.shape
    aw = ATT_HEADS * LANES
    kern = functools.partial(_dsa_kernel, tq=tq, S=T, topk=topk, iw_off=GLA_RANK, scale=LANES ** -0.5,
                             idx_scale=(LANES * IDX_HEADS) ** -0.5)
    iq_o = col3["iq"] // (IDX_HEADS * LANES)
    return pl.pallas_call(
        kern,
        grid=(B, T // tq),
        in_specs=[pl.BlockSpec((1, tq, IDX_HEADS * LANES), lambda b, i: (b, i, iq_o)),
                  pl.BlockSpec((1, T, LANES), lambda b, i: (b, 0, col2["ik"] // LANES)),
                  pl.BlockSpec((1, tq, LANES), lambda b, i: (b, i, col2["gaiw"] // LANES)),
                  pl.BlockSpec((1, tq, aw), lambda b, i: (b, i, col3["aq"] // aw)),
                  pl.BlockSpec((1, T, aw), lambda b, i: (b, 0, col2["ak"] // aw)),
                  pl.BlockSpec((1, T, aw), lambda b, i: (b, 0, col2["av"] // aw)),
                  pl.BlockSpec((1, tq, aw), lambda b, i: (b, i, col1["az"] // aw))],
        out_specs=pl.BlockSpec((1, tq, aw), lambda b, i: (b, i, 0)),
        out_shape=jax.ShapeDtypeStruct((B, T, aw), BF16),
        scratch_shapes=[pltpu.VMEM((tq, LANES), I32)],
        compiler_params=_cparams(("parallel", "arbitrary")),
        name="dsa_prompt",
    )(p3, p2b, p2f, p3, p2b, p2b, p1)


def _sample_scores_kernel(pt_ref, iq_ref, wb_ref, kidx_ref, o_ref, *, idx_scale):
    s = _dot_nt(iq_ref[0], kidx_ref[0, 0].astype(BF16))
    s = jnp.maximum(s, 0.0) * (wb_ref[0] * idx_scale)
    tq = o_ref.shape[1]
    o_ref[0] = jnp.sum(s.reshape(IDX_HEADS, tq, s.shape[-1]), axis=0)


def _sample_scores(page_table, iq_rows, wb, cache_kidx, layer):
    B, n_pages = page_table.shape
    tq = iq_rows.shape[1] // IDX_HEADS
    kern = functools.partial(_sample_scores_kernel, idx_scale=(LANES * IDX_HEADS) ** -0.5)
    return pl.pallas_call(
        kern,
        grid_spec=pltpu.PrefetchScalarGridSpec(
            num_scalar_prefetch=1,
            grid=(B, n_pages),
            in_specs=[pl.BlockSpec((1, IDX_HEADS * tq, LANES), lambda b, p, pt: (b, 0, 0)),
                      pl.BlockSpec((1, IDX_HEADS * tq, LANES), lambda b, p, pt: (b, 0, 0)),
                      pl.BlockSpec((1, 1, PAGE_SIZE, LANES), lambda b, p, pt: (layer, pt[b, p], 0, 0))],
            out_specs=pl.BlockSpec((1, tq, PAGE_SIZE), lambda b, p, pt: (b, 0, p)),
        ),
        out_shape=jax.ShapeDtypeStruct((B, tq, n_pages * PAGE_SIZE), F32),
        compiler_params=_cparams(("parallel", "arbitrary")),
        name="sample_scores",
    )(page_table, iq_rows, wb, cache_kidx)


def _sample_attn_kernel(pt_ref, sc_ref, iq_ref, wb_ref, ikn_ref, q_ref, kn_ref, vn_ref, z_ref, ck_ref, cv_ref,
                        base_ref, mrow_ref, eye_ref, o_ref,
                        kaug_ref, qs_ref, thr_ref, cut_ref, bnew_ref, m_ref, l_ref, acc_ref,
                        *, tq, past, topk, scale, idx_scale):
    p = pl.program_id(1)
    n_pages = pl.num_programs(1)
    H = ATT_HEADS
    HT = H * tq
    lane = lax.broadcasted_iota(I32, (tq, LANES), 1)
    row = lax.broadcasted_iota(I32, (tq, LANES), 0)

    @pl.when(p == 0)
    def _():
        kaug_ref[:, LANES:2 * LANES] = eye_ref[...]
        q = q_ref[0].astype(F32)
        qs_ref[...] = jnp.concatenate([q[:, h * LANES:(h + 1) * LANES] for h in range(H)], axis=0).astype(BF16)
        ikn = jnp.concatenate([ikn_ref[0], jnp.zeros((LANES - tq, LANES), F32)], axis=0).astype(BF16)
        s = jnp.maximum(_dot_nt(iq_ref[0], ikn), 0.0) * (wb_ref[0] * idx_scale)
        s_new = jnp.sum(s.reshape(IDX_HEADS, tq, LANES), axis=0)
        key_new = _sort_key(jnp.where((lane <= row) & (lane < tq), s_new, -jnp.inf))
        key_past = _sort_key(sc_ref[0])

        kk = float(topk)
        count_ge = lambda t: _count(key_past >= t) + _count(key_new >= t)
        thr = _kth_key(count_ge, kk, tq)
        thr_ref[...] = jnp.broadcast_to(thr, thr_ref.shape)
        cut_ref[...] = jnp.full(cut_ref.shape, past + LANES, I32)
        tied = jnp.where((count_ge(thr) > kk) & (thr > KEY_NEG_INF), 1.0, 0.0)

        @pl.when(jnp.max(tied) > 0.0)
        def _():
            need = kk - _count(key_past > thr) - _count(key_new > thr)
            idx_past = lax.broadcasted_iota(I32, key_past.shape, 1)
            eq_past = key_past == thr
            eq_new = key_new == thr
            cnt = lambda c_: _count(eq_past & (idx_past < c_)) + _count(eq_new & (lane + past < c_))
            c = _tie_cutoff(cnt, need, tq, int(past + LANES).bit_length())
            cut_ref[...] = jnp.broadcast_to(c, cut_ref.shape)

        sel_new = (key_new > thr) | ((key_new == thr) & (lane + past < cut_ref[...]))
        bnew_ref[...] = jnp.where(sel_new & (lane <= row) & (lane < tq), 0.0, NEG_BIG)
        m_ref[...] = jnp.full(m_ref.shape, -jnp.inf, F32)
        l_ref[...] = jnp.zeros(l_ref.shape, F32)
        acc_ref[...] = jnp.zeros(acc_ref.shape, F32)

    def online_update(logits, v_bf16):
        m_old = m_ref[:, 0:1]
        m_new = jnp.maximum(m_old, jnp.max(logits, axis=1, keepdims=True))
        alpha = jnp.exp(m_old - m_new)
        pe = jnp.exp(logits - m_new)
        l_ref[...] = jnp.broadcast_to(alpha * l_ref[:, 0:1] + jnp.sum(pe, axis=1, keepdims=True), l_ref.shape)
        acc_ref[...] = alpha * acc_ref[...] + _dot(pe.astype(BF16), v_bf16)
        m_ref[...] = jnp.broadcast_to(m_new, m_ref.shape)

    kaug_ref[:, 0:LANES] = ck_ref[0, 0].reshape(PAGE_SIZE * H, LANES).astype(BF16)
    start = pl.multiple_of(p * PAGE_SIZE, PAGE_SIZE)
    key_p = _sort_key(sc_ref[0, :, pl.ds(start, PAGE_SIZE)])
    thr = thr_ref[...]
    sel = (key_p > thr) | ((key_p == thr) & (lane + p * PAGE_SIZE < cut_ref[...]))
    bias = jnp.where(sel, 0.0, NEG_BIG)
    qa = jnp.concatenate([qs_ref[...], jnp.concatenate([bias] * H, axis=0).astype(BF16)], axis=1)
    page_shift = jnp.concatenate([mrow_ref[...] * p.astype(F32)] * H, axis=1)
    logits = _dot_nt(qa, kaug_ref[...]) * scale + base_ref[...] + page_shift
    online_update(logits, cv_ref[0, 0].reshape(PAGE_SIZE * H, LANES).astype(BF16))

    @pl.when(p == n_pages - 1)
    def _():
        zpad = jnp.zeros((LANES - tq, LANES), F32)
        bnew = bnew_ref[...]
        dist = (row - lane).astype(F32)
        qs = qs_ref[...]
        lg = []
        for h in range(H):
            cols = slice(h * LANES, (h + 1) * LANES)
            kn = jnp.concatenate([kn_ref[0, :, cols], zpad], axis=0).astype(BF16)
            qk = _dot_nt(qs, kn)[h * tq:(h + 1) * tq]
            lg.append(qk * scale - (2.0 ** -(h + 1)) * dist + bnew)
        logits_new = jnp.concatenate(lg, axis=0)
        m_old = m_ref[:, 0:1]
        m_new = jnp.maximum(m_old, jnp.max(logits_new, axis=1, keepdims=True))
        alpha = jnp.exp(m_old - m_new)
        pe = jnp.exp(logits_new - m_new)
        l_fin = alpha * l_ref[:, 0:1] + jnp.sum(pe, axis=1, keepdims=True)
        acc = alpha * acc_ref[...]
        pe = pe.astype(BF16)
        for h in range(H):
            cols = slice(h * LANES, (h + 1) * LANES)
            vn = jnp.concatenate([vn_ref[0, :, cols], zpad], axis=0).astype(BF16)
            rows = slice(h * tq, (h + 1) * tq)
            out = (acc[rows] + _dot(pe, vn)[rows]) / l_fin[rows]
            zg = z_ref[0, :, cols]
            o_ref[0, :, cols] = (out * (zg * _sigmoid(zg))).astype(o_ref.dtype)


def _sample_attn_constants(tq, past):
    H = ATT_HEADS
    hh = np.arange(H)[:, None, None, None]
    tt = np.arange(tq)[None, :, None, None]
    ss = np.arange(PAGE_SIZE)[None, None, :, None]
    h2 = np.arange(H)[None, None, None, :]
    slope = 2.0 ** -(hh + 1.0)
    base = np.where(hh == h2, -slope * (past + tt - ss), NEG_BIG)
    base = np.broadcast_to(base, (H, tq, PAGE_SIZE, H)).reshape(H * tq, PAGE_SIZE * H).astype(np.float32)
    mrow = np.broadcast_to((2.0 ** -(np.arange(H) + 1.0))[:, None, None] * PAGE_SIZE, (H, tq, LANES))
    mrow = mrow.reshape(H * tq, LANES).astype(np.float32)
    eye = np.repeat(np.eye(PAGE_SIZE, dtype=np.float32), H, axis=0)
    return jnp.asarray(base), jnp.asarray(mrow), jnp.asarray(eye, dtype=BF16)


def _sample_attn(page_table, scores, iq_rows, wb, ik_new, q, k_new, v_new, z, cache_k, cache_v, consts, layer, topk):
    B, n_pages = page_table.shape
    tq = q.shape[1]
    H = ATT_HEADS
    aw = H * LANES
    past = n_pages * PAGE_SIZE
    base, mrow, eye = consts
    kern = functools.partial(_sample_attn_kernel, tq=tq, past=past, topk=topk, scale=LANES ** -0.5,
                             idx_scale=(LANES * IDX_HEADS) ** -0.5)
    per_b = lambda b, p, pt: (b, 0, 0)
    const2 = lambda b, p, pt: (0, 0)
    page = lambda b, p, pt: (layer, pt[b, p], 0, 0, 0)
    return pl.pallas_call(
        kern,
        grid_spec=pltpu.PrefetchScalarGridSpec(
            num_scalar_prefetch=1,
            grid=(B, n_pages),
            in_specs=[pl.BlockSpec((1, tq, past), per_b),
                      pl.BlockSpec((1, IDX_HEADS * tq, LANES), per_b),
                      pl.BlockSpec((1, IDX_HEADS * tq, LANES), per_b),
                      pl.BlockSpec((1, tq, LANES), per_b),
                      pl.BlockSpec((1, tq, aw), per_b),
                      pl.BlockSpec((1, tq, aw), per_b),
                      pl.BlockSpec((1, tq, aw), per_b),
                      pl.BlockSpec((1, tq, aw), per_b),
                      pl.BlockSpec((1, 1, PAGE_SIZE, H, LANES), page),
                      pl.BlockSpec((1, 1, PAGE_SIZE, H, LANES), page),
                      pl.BlockSpec(base.shape, const2),
                      pl.BlockSpec(mrow.shape, const2),
                      pl.BlockSpec(eye.shape, const2)],
            out_specs=pl.BlockSpec((1, tq, aw), per_b),
            scratch_shapes=[pltpu.VMEM((PAGE_SIZE * H, 2 * LANES), BF16),
                            pltpu.VMEM((H * tq, LANES), BF16),
                            pltpu.VMEM((tq, LANES), I32),
                            pltpu.VMEM((tq, LANES), I32),
                            pltpu.VMEM((tq, LANES), F32),
                            pltpu.VMEM((H * tq, LANES), F32),
                            pltpu.VMEM((H * tq, LANES), F32),
                            pltpu.VMEM((H * tq, LANES), F32)],
        ),
        out_shape=jax.ShapeDtypeStruct((B, tq, aw), BF16),
        compiler_params=_cparams(("parallel", "arbitrary")),
        name="sample_attention",
    )(page_table, scores, iq_rows, wb, ik_new, q, k_new, v_new, z, cache_k, cache_v, base, mrow, eye)


def _outproj_kernel(oa_ref, ob_ref, ma_ref, mb_ref, x_ref, gate_ref, wpa_ref, wpb_ref, wo_ref, gf_ref, o_ref, *, final):
    pa = _dot(oa_ref[0], wpa_ref[...])
    pb = _dot(ob_ref[0], wpb_ref[...])
    merged = _sigmoid(ma_ref[0]) * pa + _sigmoid(mb_ref[0]) * pb
    x = x_ref[0] + gate_ref[0] * _dot(merged.astype(BF16), wo_ref[...])
    if final:
        x = x * lax.rsqrt(jnp.mean(x * x, axis=-1, keepdims=True) + EPS) * gf_ref[...]
    o_ref[0] = x


def _outproj(o_a, o_b, p1, col1, x, gate, wpa, wpb, wo, g_final, final, tm):
    B, T, D = x.shape
    R = gate.shape[1]
    rb = 1 if R == 1 else tm
    gate_map = (lambda b, i: (b, 0, 0)) if R == 1 else (lambda b, i: (b, i, 0))
    const = lambda b, i: (0, 0)
    once = pl.Buffered(1)
    return pl.pallas_call(
        functools.partial(_outproj_kernel, final=final),
        grid=(B, T // tm),
        in_specs=[pl.BlockSpec((1, tm, o_a.shape[2]), lambda b, i: (b, i, 0)),
                  pl.BlockSpec((1, tm, o_b.shape[2]), lambda b, i: (b, i, 0)),
                  pl.BlockSpec((1, tm, D), lambda b, i: (b, i, col1["m_a"] // D)),
                  pl.BlockSpec((1, tm, D), lambda b, i: (b, i, col1["m_b"] // D)),
                  pl.BlockSpec((1, tm, D), lambda b, i: (b, i, 0)),
                  pl.BlockSpec((1, rb, D), gate_map),
                  pl.BlockSpec(wpa.shape, const, pipeline_mode=once),
                  pl.BlockSpec(wpb.shape, const, pipeline_mode=once),
                  pl.BlockSpec(wo.shape, const, pipeline_mode=once),
                  pl.BlockSpec((1, D), const)],
        out_specs=pl.BlockSpec((1, tm, D), lambda b, i: (b, i, 0)),
        out_shape=jax.ShapeDtypeStruct((B, T, D), F32),
        compiler_params=_cparams(("parallel", "parallel")),
        name="merge_outproj",
    )(o_a, o_b, p1, p1, x, gate, wpa, wpb, wo, g_final.reshape(1, D))


def _column_groups(d):
    gqk, gv, att, iqw = d // 4, d // 2, d // 2, d
    sizes = [("gq", gqk), ("gk", gqk), ("gv", gv), ("gz", gv), ("ga", GLA_RANK), ("aq", att), ("ak", att),
             ("av", att), ("az", att), ("iq", iqw), ("ik", LANES), ("iw", IDX_HEADS), ("m_a", d), ("m_b", d)]
    src, o = {}, 0
    for n, s in sizes:
        src[n] = (o, o + s)
        o += s
    groups = (("gq", "gk", "gv", "gz", "az", "m_a", "m_b"), ("ak", "av", "ik", "gaiw"), ("iq", "aq"))
    return src, groups


def _regroup_weights(w, src, names):
    parts, offs, o = [], {}, 0
    for n in names:
        if n == "gaiw":
            pad = LANES - GLA_RANK - IDX_HEADS
            part = jnp.concatenate([w[:, src["ga"][0]:src["ga"][1]], w[:, src["iw"][0]:src["iw"][1]],
                                    jnp.zeros((w.shape[0], pad), w.dtype)], axis=1)
        else:
            part = w[:, src[n][0]:src[n][1]]
        offs[n] = o
        o += part.shape[1]
        parts.append(part)
    return jnp.concatenate(parts, axis=1).astype(BF16), offs


def kernel(x_prompt, x_sample, cache_k, cache_v, cache_kidx, state_gla, page_table, c_prompt, c_sample, w_c, b_c,
           g_norm, w_in, w_a2, b_a2, g_gla, w_pa, w_pb, w_out, g_final):
    depth = w_in.shape[0]
    B, T, D = x_prompt.shape
    Bs, Ts, _ = x_sample.shape
    H, dk, dv = state_gla.shape[2], state_gla.shape[3], state_gla.shape[4]
    n_pages = page_table.shape[1]
    past = n_pages * PAGE_SIZE
    aw = ATT_HEADS * LANES
    src, groups = _column_groups(D)

    rows = -(-(B + Bs) // SUBLANES) * SUBLANES
    c_all = jnp.concatenate([c_prompt, c_sample, jnp.zeros((rows - B - Bs, D), F32)], axis=0)
    mod = _modulation(c_all, w_c, b_c)

    gla_consts = _gla_constants(GLA_CHUNK)
    attn_consts = _sample_attn_constants(Ts, past)
    topk_p = min(TOPK_MAX, T // 4)
    topk_s = min(TOPK_MAX, (past + Ts) // 4)
    ts_pad = GLA_CHUNK

    xp, xs = x_prompt, x_sample.reshape(1, Bs * Ts, D)
    outs = {n: [] for n in ("kp", "vp", "ikp", "sp", "ks", "vs", "iks", "ss")}
    for l in range(depth):
        ws = [_regroup_weights(w_in[l], src, names) for names in groups]
        (w1, col1), (w2, col2), (w3, col3) = ws
        wa = jnp.concatenate([w_a2[l], jnp.zeros((LANES - GLA_RANK, w_a2.shape[2]), F32)], axis=0).astype(BF16)
        ba = b_a2[l].reshape(1, -1)
        wpa, wpb, wo = w_pa[l].astype(BF16), w_pb[l].astype(BF16), w_out[l].astype(BF16)
        final = l == depth - 1
        shift, scale, gate = (mod[l, :, i * D:(i + 1) * D] for i in range(3))

        sc, sh, gt = (a[:B].reshape(B, 1, D) for a in (scale, shift, gate))
        tm = min(T, 1024)
        (p1,) = _inproj(xp, sc, sh, g_norm[l], w1, (F32,), tm, 1024, "inproj_gates")
        p2f, p2b = _inproj(xp, sc, sh, g_norm[l], w2, (F32, BF16), tm, 1152, "inproj_kv")
        (p3,) = _inproj(xp, sc, sh, g_norm[l], w3, (BF16,), tm, 1024, "inproj_queries")
        s0 = jnp.zeros((B, H, dk, dv), F32)
        o_a, s_new = _gla(p1, p2f, {**col1, **col2}, s0, wa, ba, g_gla[l], gla_consts, 256, None)
        o_b = _dsa_prompt(p1, p2f, p2b, p3, col1, col2, col3, topk_p, 128)
        xp = _outproj(o_a, o_b, p1, col1, xp, gt, wpa, wpb, wo, g_final, final, 256)
        outs["kp"].append(p2f[:, :, col2["ak"]:col2["ak"] + aw].reshape(B, T, ATT_HEADS, LANES))
        outs["vp"].append(p2f[:, :, col2["av"]:col2["av"] + aw].reshape(B, T, ATT_HEADS, LANES))
        outs["ikp"].append(p2f[:, :, col2["ik"]:col2["ik"] + LANES])
        outs["sp"].append(s_new)

        n_s = Bs * Ts
        sc, sh, gt = (jnp.repeat(a[B:B + Bs], Ts, axis=0).reshape(1, n_s, D) for a in (scale, shift, gate))
        (q1,) = _inproj(xs, sc, sh, g_norm[l], w1, (F32,), n_s, 1024, "inproj_gates_s")
        (q2,) = _inproj(xs, sc, sh, g_norm[l], w2, (F32,), n_s, 1152, "inproj_kv_s")
        (q3,) = _inproj(xs, sc, sh, g_norm[l], w3, (BF16,), n_s, 1024, "inproj_queries_s")
        pad_t = lambda a: jnp.pad(a.reshape(Bs, Ts, -1), ((0, 0), (0, ts_pad - Ts), (0, 0)))
        o_a_s, s_new_s = _gla(pad_t(q1[0, :, :col1["az"]]), pad_t(q2[0]), {**col1, **col2}, state_gla[l], wa, ba,
                              g_gla[l], gla_consts, ts_pad, Ts)
        o_a_s = o_a_s[:, :Ts].reshape(1, n_s, -1)

        iq = q3[0, :, col3["iq"]:col3["iq"] + IDX_HEADS * LANES].reshape(Bs, Ts, IDX_HEADS, LANES)
        iq_rows = jnp.transpose(iq, (0, 2, 1, 3)).reshape(Bs, IDX_HEADS * Ts, LANES)
        iw = q2[0, :, col2["gaiw"] + GLA_RANK:col2["gaiw"] + GLA_RANK + IDX_HEADS].reshape(Bs, Ts, IDX_HEADS)
        wb = jnp.broadcast_to(jnp.transpose(iw, (0, 2, 1)).reshape(Bs, IDX_HEADS * Ts, 1), (Bs, IDX_HEADS * Ts, LANES))
        ik_new = q2[0, :, col2["ik"]:col2["ik"] + LANES].reshape(Bs, Ts, LANES)
        k_new = q2[0, :, col2["ak"]:col2["ak"] + aw].reshape(Bs, Ts, aw)
        v_new = q2[0, :, col2["av"]:col2["av"] + aw].reshape(Bs, Ts, aw)
        aq = q3[0, :, col3["aq"]:col3["aq"] + aw].reshape(Bs, Ts, aw)
        az = q1[0, :, col1["az"]:col1["az"] + aw].reshape(Bs, Ts, aw)
        scores = _sample_scores(page_table, iq_rows, wb, cache_kidx, l)
        o_b_s = _sample_attn(page_table, scores, iq_rows, wb, ik_new, aq, k_new, v_new, az, cache_k, cache_v,
                             attn_consts, l, topk_s)
        xs = _outproj(o_a_s, o_b_s.reshape(1, n_s, aw), q1, col1, xs, gt, wpa, wpb, wo, g_final, final, n_s)
        outs["ks"].append(k_new.reshape(Bs, Ts, ATT_HEADS, LANES))
        outs["vs"].append(v_new.reshape(Bs, Ts, ATT_HEADS, LANES))
        outs["iks"].append(ik_new)
        outs["ss"].append(s_new_s)

    st = lambda n: jnp.stack(outs[n])
    return (xp, xs.reshape(Bs, Ts, D), st("kp"), st("vp"), st("ikp"), st("sp"),
            st("ks"), st("vs"), st("iks"), st("ss"))
```

```python
import functools

import numpy as np
import jax
import jax.numpy as jnp
from jax import lax
from jax.experimental import pallas as pl
from jax.experimental.pallas import tpu as pltpu

F32 = jnp.float32
BF16 = jnp.bfloat16
I32 = jnp.int32

EPS = 1e-6
PAGE_SIZE = 128
GLA_HEADS = 4
GLA_RANK = 16
GLA_TAU = 16.0
GLA_CHUNK = 64
ATT_HEADS = 8
IDX_HEADS = 16
TOPK_MAX = 256

LANES = 128
SUBLANES = 8
VMEM_LIMIT_BYTES = 56 * 1024 * 1024

NEG_BIG = -1e30
INT_MIN = -2147483648
KEY_NEG_INF = -2139095041


def _cparams(sem):
    return pltpu.CompilerParams(dimension_semantics=sem, vmem_limit_bytes=VMEM_LIMIT_BYTES)


def _dot(a, b):
    return jnp.dot(a, b, preferred_element_type=F32)


def _dot_nt(a, b):
    return lax.dot_general(a, b, (((1,), (1,)), ((), ())), preferred_element_type=F32)


def _sigmoid(x):
    return 1.0 / (1.0 + jnp.exp(-x))


def _sort_key(x):
    bits = pltpu.bitcast(x + 0.0, I32)
    return jnp.where(bits >= 0, bits, bits ^ jnp.int32(0x7FFFFFFF))


def _mod_kernel(c_ref, w_ref, b_ref, o_ref):
    o_ref[0] = _dot(c_ref[...].astype(BF16), w_ref[0].astype(BF16)) + b_ref[0]


def _modulation(c_all, w_c, b_c):
    depth, d, n = w_c.shape
    rows = c_all.shape[0]
    tn = 768
    return pl.pallas_call(
        _mod_kernel,
        grid=(depth, n // tn),
        in_specs=[pl.BlockSpec((rows, d), lambda l, j: (0, 0)),
                  pl.BlockSpec((1, d, tn), lambda l, j: (l, 0, j)),
                  pl.BlockSpec((1, 1, tn), lambda l, j: (l, 0, j))],
        out_specs=pl.BlockSpec((1, rows, tn), lambda l, j: (l, 0, j)),
        out_shape=jax.ShapeDtypeStruct((depth, rows, n), F32),
        compiler_params=_cparams(("parallel", "parallel")),
        name="adaln_modulation",
    )(c_all, w_c, b_c.reshape(depth, 1, n))


def _inproj_kernel(x_ref, sc_ref, sh_ref, g_ref, w_ref, *rest):
    out_refs, h_ref = rest[:-1], rest[-1]

    @pl.when(pl.program_id(2) == 0)
    def _():
        x = x_ref[0]
        y = x * lax.rsqrt(jnp.mean(x * x, axis=-1, keepdims=True) + EPS) * g_ref[...]
        h_ref[...] = (y * (1.0 + sc_ref[0]) + sh_ref[0]).astype(BF16)

    r = _dot(h_ref[...], w_ref[...])
    for o in out_refs:
        o[0] = r.astype(o.dtype)


def _inproj(x, scale, shift, g, w, out_dtypes, tm, tn, name):
    B, T, D = x.shape
    N = w.shape[1]
    R = scale.shape[1]
    rb = 1 if R == 1 else tm
    mod_map = (lambda b, i, j: (b, 0, 0)) if R == 1 else (lambda b, i, j: (b, i, 0))
    return pl.pallas_call(
        _inproj_kernel,
        grid=(B, T // tm, N // tn),
        in_specs=[pl.BlockSpec((1, tm, D), lambda b, i, j: (b, i, 0)),
                  pl.BlockSpec((1, rb, D), mod_map),
                  pl.BlockSpec((1, rb, D), mod_map),
                  pl.BlockSpec((1, D), lambda b, i, j: (0, 0)),
                  pl.BlockSpec((D, tn), lambda b, i, j: (0, j))],
        out_specs=[pl.BlockSpec((1, tm, tn), lambda b, i, j: (b, i, j)) for _ in out_dtypes],
        out_shape=[jax.ShapeDtypeStruct((B, T, N), dt) for dt in out_dtypes],
        scratch_shapes=[pltpu.VMEM((tm, D), BF16)],
        compiler_params=_cparams(("parallel", "parallel", "arbitrary")),
        name=name,
    )(x, scale, shift, g.reshape(1, D), w)


def _gla_constants(C):
    r = np.arange(C)[:, None]
    j = np.arange(C)[None, :]
    blocks = [(j <= r), (j > r)]
    masks = []
    w = C // 2
    while w >= 1:
        mid = (r // (2 * w)) * (2 * w) + w
        upper = r >= mid
        blocks.append(np.where(upper, (j >= mid) & (j <= r), (j > r) & (j < mid)))
        t, s = r, j
        same = (t // (2 * w)) == (s // (2 * w))
        t_up = t >= (t // (2 * w)) * (2 * w) + w
        s_lo = s < (s // (2 * w)) * (2 * w) + w
        masks.append(same & t_up & s_lo)
        w //= 2
    masks.append(r == j)
    W = np.concatenate(blocks, axis=0).astype(np.float32)
    M = np.stack(masks).astype(np.float32)
    return jnp.asarray(W, dtype=BF16), jnp.asarray(M, dtype=F32)


def _gla_kernel(q_ref, k_ref, v_ref, z_ref, ga_ref, wa_ref, ba_ref, gg_ref, s0_ref, W_ref, M_ref,
                o_ref, sout_ref, S_ref, *, C, nchunk, nlev, valid, H, dk, dv):
    ci = pl.program_id(1)

    @pl.when(ci == 0)
    def _():
        S_ref[...] = s0_ref[0]

    W = W_ref[...]
    for c in range(nchunk):
        rows = pl.ds(c * C, C)
        zz = _dot(ga_ref[0, rows, :].astype(BF16), wa_ref[...]) + ba_ref[...]
        g = -(jnp.maximum(-zz, 0.0) + jnp.log1p(jnp.exp(-jnp.abs(zz)))) * (1.0 / GLA_TAU)
        if valid is not None:
            tok = ci * (nchunk * C) + c * C + lax.broadcasted_iota(I32, g.shape, 0)
            g = jnp.where(tok < valid, g, 0.0)
        g1 = g.astype(BF16)
        r1 = g - g1.astype(F32)
        g2 = r1.astype(BF16)
        g3 = (r1 - g2.astype(F32)).astype(BF16)
        E_all = jnp.exp(_dot(W, g1) + _dot(W, g2) + _dot(W, g3))
        for h in range(H):
            kc = slice(h * dk, (h + 1) * dk)
            vc = slice(h * dv, (h + 1) * dv)
            E = E_all[:, kc]
            qs = q_ref[0, rows, kc] * (dk ** -0.5)
            k = k_ref[0, rows, kc]
            v = v_ref[0, rows, vc].astype(BF16)
            e_b = E[0:C]
            e_rest = E[C:2 * C]
            A = jnp.where(M_ref[nlev] > 0, _dot_nt(qs.astype(BF16), k.astype(BF16)), 0.0)
            for l in range(nlev):
                e_l = E[(2 + l) * C:(3 + l) * C]
                A = A + jnp.where(M_ref[l] > 0, _dot_nt((qs * e_l).astype(BF16), (k * e_l).astype(BF16)), 0.0)
            S = S_ref[h]
            o = _dot(A.astype(BF16), v) + _dot((qs * e_b).astype(BF16), S.astype(BF16))
            k_dec = k * e_rest
            e_last = e_b.T[:, C - 1:C]
            S_ref[h] = e_last * S + _dot(k_dec.T.astype(BF16), v)
            y = o * lax.rsqrt(jnp.mean(o * o, axis=-1, keepdims=True) + EPS) * gg_ref[h:h + 1, :]
            zg = z_ref[0, rows, vc]
            o_ref[0, rows, vc] = (y * (zg * _sigmoid(zg))).astype(o_ref.dtype)

    @pl.when(ci == pl.num_programs(1) - 1)
    def _():
        sout_ref[0] = S_ref[...]


def _gla(p1, p2, col, s0, wa, ba, gg, consts, tc, valid):
    B, T, _ = p1.shape
    H, dk, dv = s0.shape[1], s0.shape[2], s0.shape[3]
    W, M = consts
    C = GLA_CHUNK
    nlev = M.shape[0] - 1
    qo, ko, vo, zo = (col[n] // (H * w) for n, w in (("gq", dk), ("gk", dk), ("gv", dv), ("gz", dv)))
    gao = col["gaiw"] // LANES
    kern = functools.partial(_gla_kernel, C=C, nchunk=tc // C, nlev=nlev, valid=valid, H=H, dk=dk, dv=dv)
    return pl.pallas_call(
        kern,
        grid=(B, T // tc),
        in_specs=[pl.BlockSpec((1, tc, H * dk), lambda b, i: (b, i, qo)),
                  pl.BlockSpec((1, tc, H * dk), lambda b, i: (b, i, ko)),
                  pl.BlockSpec((1, tc, H * dv), lambda b, i: (b, i, vo)),
                  pl.BlockSpec((1, tc, H * dv), lambda b, i: (b, i, zo)),
                  pl.BlockSpec((1, tc, LANES), lambda b, i: (b, i, gao)),
                  pl.BlockSpec((LANES, H * dk), lambda b, i: (0, 0)),
                  pl.BlockSpec((1, H * dk), lambda b, i: (0, 0)),
                  pl.BlockSpec((H, dv), lambda b, i: (0, 0)),
                  pl.BlockSpec((1, H, dk, dv), lambda b, i: (b, 0, 0, 0)),
                  pl.BlockSpec(W.shape, lambda b, i: (0, 0)),
                  pl.BlockSpec(M.shape, lambda b, i: (0, 0, 0))],
        out_specs=[pl.BlockSpec((1, tc, H * dv), lambda b, i: (b, i, 0)),
                   pl.BlockSpec((1, H, dk, dv), lambda b, i: (b, 0, 0, 0))],
        out_shape=[jax.ShapeDtypeStruct((B, T, H * dv), BF16),
                   jax.ShapeDtypeStruct((B, H, dk, dv), F32)],
        scratch_shapes=[pltpu.VMEM((H, dk, dv), F32)],
        compiler_params=_cparams(("parallel", "arbitrary")),
        name="gla_branch",
    )(p1, p1, p1, p1, p2, wa, ba, gg, s0, W, M)


def _kth_key(count_ge, kk, rows):
    def body(i, u):
        cand = u | (jnp.int32(1) << (31 - i))
        ok = count_ge(cand ^ jnp.int32(INT_MIN)) >= kk
        return jnp.where(ok, cand, u)
    u = lax.fori_loop(0, 32, body, jnp.zeros((rows, 1), I32))
    return u ^ jnp.int32(INT_MIN)


def _kth_key_groups(keys, kk):
    def body(i, us):
        bit = jnp.int32(1) << (31 - i)
        cands = [u | bit for u in us]
        oks = [_count(key >= (c ^ jnp.int32(INT_MIN))) >= kk for key, c in zip(keys, cands)]
        return tuple(jnp.where(ok, c, u) for ok, c, u in zip(oks, cands, us))
    us = lax.fori_loop(0, 32, body, tuple(jnp.zeros((k.shape[0], 1), I32) for k in keys), unroll=2)
    return jnp.concatenate([u ^ jnp.int32(INT_MIN) for u in us], axis=0)


def _tie_cutoff(count_eq_below, need, rows, nbits):
    def body(i, c):
        cand = c | (jnp.int32(1) << (nbits - 1 - i))
        ok = count_eq_below(cand) <= need
        return jnp.where(ok, cand, c)
    return lax.fori_loop(0, nbits, body, jnp.zeros((rows, 1), I32))


def _count(mask):
    return jnp.sum(jnp.where(mask, 1.0, 0.0), axis=1, keepdims=True)


def _dsa_block(iq_ref, ik_ref, iw_ref, q_ref, k_ref, v_ref, z_ref, o_ref, bias_ref, *, S, tq, topk, iw_off, scale,
               idx_scale):
    q0 = pl.program_id(1) * tq
    pos = q0 + lax.broadcasted_iota(I32, (tq, S), 0)
    kpos = lax.broadcasted_iota(I32, (tq, S), 1)

    ik = ik_ref[0, 0:S, :]
    iw = iw_ref[0]
    score = jnp.zeros((tq, S), F32)
    for h in range(IDX_HEADS):
        s = _dot_nt(iq_ref[0, :, h * LANES:(h + 1) * LANES], ik)
        score = score + jnp.maximum(s, 0.0) * (iw[:, iw_off + h:iw_off + h + 1] * idx_scale)
    key = _sort_key(jnp.where(kpos <= pos, score, -jnp.inf))

    kk = float(topk)
    thr = _kth_key_groups([key[0:tq // 2], key[tq // 2:tq]], kk)
    thr_sel = jnp.maximum(thr, KEY_NEG_INF + 1)
    bias_ref[:, 0:S] = jnp.where(key >= thr_sel, 0.0, NEG_BIG)
    tied = jnp.where(_count(key >= thr_sel) > kk, 1.0, 0.0)

    @pl.when(jnp.max(tied) > 0.0)
    def _():
        need = kk - _count(key > thr)
        eq_pos = jnp.where(key == thr, kpos, jnp.int32(2 ** 30))
        cut = _tie_cutoff(lambda c_: _count(eq_pos < c_), need, tq, int(S).bit_length())
        keep_eq = jnp.where(key == thr, jnp.where(kpos < cut, 0.0, NEG_BIG), NEG_BIG)
        bias_ref[:, 0:S] = jnp.where(key >= thr_sel, jnp.where(key > thr, 0.0, keep_eq), NEG_BIG)

    log2e = 1.4426950408889634
    krel = (lax.broadcasted_iota(I32, (1, S), 1) - q0).astype(F32)
    for h in range(ATT_HEADS):
        cols = slice(h * LANES, (h + 1) * LANES)
        qk = _dot_nt(q_ref[0, :, cols], k_ref[0, 0:S, cols])
        logits = qk * (scale * log2e) + krel * ((2.0 ** -(h + 1)) * log2e) + bias_ref[:, 0:S]
        m = jnp.max(logits, axis=1, keepdims=True)
        p = jnp.exp2(logits - m)
        l = jnp.sum(p, axis=1, keepdims=True)
        out = _dot(p.astype(BF16), v_ref[0, 0:S, cols]) / l
        zg = z_ref[0, :, cols]
        o_ref[0, :, cols] = (out * (zg * _sigmoid(zg))).astype(o_ref.dtype)


def _dsa_kernel(*refs, tq, T, groups, **kw):
    qi = pl.program_id(1)
    per = (T // tq) // groups
    for g in range(groups):
        pl.when(qi // per == g)(functools.partial(_dsa_block, *refs, S=(g + 1) * per * tq, tq=tq, **kw))


def _dsa_prompt(p1, p2f, p2b, p3, col1, col2, col3, topk, tq):
    B, T, _ = p1.shape
    aw = ATT_HEADS * LANES
    groups = min(4, T // tq)
    kern = functools.partial(_dsa_kernel, tq=tq, T=T, groups=groups, topk=topk, iw_off=GLA_RANK, scale=LANES ** -0.5,
                             idx_scale=(LANES * IDX_HEADS) ** -0.5)
    iq_o = col3["iq"] // (IDX_HEADS * LANES)
    return pl.pallas_call(
        kern,
        grid=(B, T // tq),
        in_specs=[pl.BlockSpec((1, tq, IDX_HEADS * LANES), lambda b, i: (b, i, iq_o)),
                  pl.BlockSpec((1, T, LANES), lambda b, i: (b, 0, col2["ik"] // LANES)),
                  pl.BlockSpec((1, tq, LANES), lambda b, i: (b, i, col2["gaiw"] // LANES)),
                  pl.BlockSpec((1, tq, aw), lambda b, i: (b, i, col3["aq"] // aw)),
                  pl.BlockSpec((1, T, aw), lambda b, i: (b, 0, col2["ak"] // aw)),
                  pl.BlockSpec((1, T, aw), lambda b, i: (b, 0, col2["av"] // aw)),
                  pl.BlockSpec((1, tq, aw), lambda b, i: (b, i, col1["az"] // aw))],
        out_specs=pl.BlockSpec((1, tq, aw), lambda b, i: (b, i, 0)),
        out_shape=jax.ShapeDtypeStruct((B, T, aw), BF16),
        scratch_shapes=[pltpu.VMEM((tq, T), F32)],
        compiler_params=_cparams(("parallel", "arbitrary")),
        name="dsa_prompt",
    )(p3, p2b, p2f, p3, p2b, p2b, p1)


def _page_specs(n, block, layer, pages_per_step):
    def spec(g):
        return pl.BlockSpec(block, lambda b, p, pt: (layer, pt[b, p * pages_per_step + g]) + (0,) * (len(block) - 2))
    return [spec(g) for g in range(n)]


def _sample_scores_kernel(pt_ref, iq_ref, wb_ref, *rest, idx_scale):
    kidx_refs, o_ref = rest[:-1], rest[-1]
    iq = iq_ref[0]
    w = wb_ref[0] * idx_scale
    tq = o_ref.shape[1]
    for g, kidx_ref in enumerate(kidx_refs):
        s = jnp.maximum(_dot_nt(iq, kidx_ref[...].astype(BF16)), 0.0) * w
        o_ref[0, :, g * PAGE_SIZE:(g + 1) * PAGE_SIZE] = jnp.sum(s.reshape(IDX_HEADS, tq, PAGE_SIZE), axis=0)


def _sample_scores(page_table, iq_rows, wb, cache_kidx, layer, pages_per_step):
    B, n_pages = page_table.shape
    tq = iq_rows.shape[1] // IDX_HEADS
    G = pages_per_step
    kern = functools.partial(_sample_scores_kernel, idx_scale=(LANES * IDX_HEADS) ** -0.5)
    return pl.pallas_call(
        kern,
        grid_spec=pltpu.PrefetchScalarGridSpec(
            num_scalar_prefetch=1,
            grid=(B, n_pages // G),
            in_specs=[pl.BlockSpec((1, IDX_HEADS * tq, LANES), lambda b, p, pt: (b, 0, 0)),
                      pl.BlockSpec((1, IDX_HEADS * tq, LANES), lambda b, p, pt: (b, 0, 0))]
                     + _page_specs(G, (None, None, PAGE_SIZE, LANES), layer, G),
            out_specs=pl.BlockSpec((1, tq, G * PAGE_SIZE), lambda b, p, pt: (b, 0, p)),
        ),
        out_shape=jax.ShapeDtypeStruct((B, tq, n_pages * PAGE_SIZE), F32),
        compiler_params=_cparams(("parallel", "arbitrary")),
        name="sample_scores",
    )(page_table, iq_rows, wb, *([cache_kidx] * G))


def _sample_attn_kernel(pt_ref, sc_ref, iq_ref, wb_ref, ikn_ref, q_ref, kn_ref, vn_ref, z_ref, *rest,
                        G, tq, past, topk, scale, idx_scale):
    ck_refs, cv_refs, o_ref = rest[:G], rest[G:2 * G], rest[2 * G]
    qs_ref, thr_ref, cut_ref, bnew_ref, m_ref, l_ref, acc_ref = rest[2 * G + 1:]
    p = pl.program_id(1)
    H = ATT_HEADS
    lane = lax.broadcasted_iota(I32, (tq, LANES), 1)
    row = lax.broadcasted_iota(I32, (tq, LANES), 0)

    @pl.when(p == 0)
    def _():
        q = q_ref[0].astype(F32)
        qs_ref[...] = jnp.concatenate([q[:, h * LANES:(h + 1) * LANES] for h in range(H)], axis=0)
        ikn = jnp.concatenate([ikn_ref[0], jnp.zeros((LANES - tq, LANES), F32)], axis=0).astype(BF16)
        s = jnp.maximum(_dot_nt(iq_ref[0], ikn), 0.0) * (wb_ref[0] * idx_scale)
        s_new = jnp.sum(s.reshape(IDX_HEADS, tq, LANES), axis=0)
        key_new = _sort_key(jnp.where((lane <= row) & (lane < tq), s_new, -jnp.inf))
        key_past = _sort_key(sc_ref[0])

        kk = float(topk)
        count_ge = lambda t: _count(key_past >= t) + _count(key_new >= t)
        thr = _kth_key(count_ge, kk, tq)
        thr_ref[...] = jnp.broadcast_to(thr, thr_ref.shape)
        cut_ref[...] = jnp.full(cut_ref.shape, past + LANES, I32)
        tied = jnp.where((count_ge(thr) > kk) & (thr > KEY_NEG_INF), 1.0, 0.0)

        @pl.when(jnp.max(tied) > 0.0)
        def _():
            need = kk - _count(key_past > thr) - _count(key_new > thr)
            idx_past = lax.broadcasted_iota(I32, key_past.shape, 1)
            eq_past = key_past == thr
            eq_new = key_new == thr
            cnt = lambda c_: _count(eq_past & (idx_past < c_)) + _count(eq_new & (lane + past < c_))
            c = _tie_cutoff(cnt, need, tq, int(past + LANES).bit_length())
            cut_ref[...] = jnp.broadcast_to(c, cut_ref.shape)

        sel_new = (key_new > thr) | ((key_new == thr) & (lane + past < cut_ref[...]))
        bnew_ref[...] = jnp.where(sel_new & (lane <= row) & (lane < tq), 0.0, NEG_BIG)
        m_ref[...] = jnp.full(m_ref.shape, -jnp.inf, F32)
        l_ref[...] = jnp.zeros(l_ref.shape, F32)
        acc_ref[...] = jnp.zeros(acc_ref.shape, F32)

    thr = thr_ref[...]
    cut = cut_ref[...]
    bias, dist = [], []
    for g in range(G):
        first = pl.multiple_of((p * G + g) * PAGE_SIZE, PAGE_SIZE)
        key_p = _sort_key(sc_ref[0, :, pl.ds(first, PAGE_SIZE)])
        kpos = lane + first
        keep_eq = jnp.where(key_p == thr, jnp.where(kpos < cut, 0.0, NEG_BIG), NEG_BIG)
        bias.append(jnp.where(key_p > thr, 0.0, keep_eq))
        dist.append((row + past - kpos).astype(F32))
    head_rows = [pl.ds(h, PAGE_SIZE, stride=H) for h in range(H)]
    qs = qs_ref[...]
    logits = []
    for h in range(H):
        q_h = qs[h * tq:(h + 1) * tq].astype(BF16)
        logits.append([_dot_nt(q_h, ck_refs[g][head_rows[h], :].astype(BF16)) * scale
                       - (2.0 ** -(h + 1)) * dist[g] + bias[g] for g in range(G)])
    m_old = m_ref[:, 0:1]
    page_max = jnp.concatenate([functools.reduce(jnp.maximum, lg) for lg in logits], axis=0)
    m_new = jnp.maximum(m_old, jnp.max(page_max, axis=1, keepdims=True))
    alpha = jnp.exp(m_old - m_new)
    pe = [[jnp.exp(logits[h][g] - m_new[h * tq:(h + 1) * tq]) for g in range(G)] for h in range(H)]
    psum = jnp.concatenate([functools.reduce(jnp.add, ph) for ph in pe], axis=0)
    pv = [functools.reduce(jnp.add, [_dot(pe[h][g].astype(BF16), cv_refs[g][head_rows[h], :].astype(BF16))
                                     for g in range(G)]) for h in range(H)]
    l_new = alpha * l_ref[:, 0:1] + jnp.sum(psum, axis=1, keepdims=True)
    acc_ref[...] = alpha * acc_ref[...] + jnp.concatenate(pv, axis=0)
    l_ref[...] = jnp.broadcast_to(l_new, l_ref.shape)
    m_ref[...] = jnp.broadcast_to(m_new, m_ref.shape)

    @pl.when(p == pl.num_programs(1) - 1)
    def _():
        zpad = jnp.zeros((LANES - tq, LANES), F32)
        bnew = bnew_ref[...]
        dist = (row - lane).astype(F32)
        qs = qs_ref[...].astype(BF16)
        lg = []
        for h in range(H):
            cols = slice(h * LANES, (h + 1) * LANES)
            kn = jnp.concatenate([kn_ref[0, :, cols], zpad], axis=0).astype(BF16)
            qk = _dot_nt(qs, kn)[h * tq:(h + 1) * tq]
            lg.append(qk * scale - (2.0 ** -(h + 1)) * dist + bnew)
        logits_new = jnp.concatenate(lg, axis=0)
        m_old = m_ref[:, 0:1]
        m_new = jnp.maximum(m_old, jnp.max(logits_new, axis=1, keepdims=True))
        alpha = jnp.exp(m_old - m_new)
        pe = jnp.exp(logits_new - m_new)
        l_fin = alpha * l_ref[:, 0:1] + jnp.sum(pe, axis=1, keepdims=True)
        acc = alpha * acc_ref[...]
        pe = pe.astype(BF16)
        for h in range(H):
            cols = slice(h * LANES, (h + 1) * LANES)
            vn = jnp.concatenate([vn_ref[0, :, cols], zpad], axis=0).astype(BF16)
            rows = slice(h * tq, (h + 1) * tq)
            out = (acc[rows] + _dot(pe, vn)[rows]) / l_fin[rows]
            zg = z_ref[0, :, cols]
            o_ref[0, :, cols] = (out * (zg * _sigmoid(zg))).astype(o_ref.dtype)


def _sample_attn(page_table, scores, iq_rows, wb, ik_new, q, k_new, v_new, z, cache_k, cache_v, layer, topk,
                 pages_per_step):
    B, n_pages = page_table.shape
    tq = q.shape[1]
    H = ATT_HEADS
    aw = H * LANES
    past = n_pages * PAGE_SIZE
    G = pages_per_step
    kern = functools.partial(_sample_attn_kernel, G=G, tq=tq, past=past, topk=topk, scale=LANES ** -0.5,
                             idx_scale=(LANES * IDX_HEADS) ** -0.5)
    per_b = lambda b, p, pt: (b, 0, 0)
    page_block = (None, None, PAGE_SIZE * H, LANES)
    cache_k, cache_v = (c.reshape(c.shape[0], c.shape[1], PAGE_SIZE * H, LANES) for c in (cache_k, cache_v))
    return pl.pallas_call(
        kern,
        grid_spec=pltpu.PrefetchScalarGridSpec(
            num_scalar_prefetch=1,
            grid=(B, n_pages // G),
            in_specs=[pl.BlockSpec((1, tq, past), per_b),
                      pl.BlockSpec((1, IDX_HEADS * tq, LANES), per_b),
                      pl.BlockSpec((1, IDX_HEADS * tq, LANES), per_b),
                      pl.BlockSpec((1, tq, LANES), per_b),
                      pl.BlockSpec((1, tq, aw), per_b),
                      pl.BlockSpec((1, tq, aw), per_b),
                      pl.BlockSpec((1, tq, aw), per_b),
                      pl.BlockSpec((1, tq, aw), per_b)]
                     + _page_specs(G, page_block, layer, G) + _page_specs(G, page_block, layer, G),
            out_specs=pl.BlockSpec((1, tq, aw), per_b),
            scratch_shapes=[pltpu.VMEM((H * tq, LANES), F32),
                            pltpu.VMEM((tq, LANES), I32),
                            pltpu.VMEM((tq, LANES), I32),
                            pltpu.VMEM((tq, LANES), F32),
                            pltpu.VMEM((H * tq, LANES), F32),
                            pltpu.VMEM((H * tq, LANES), F32),
                            pltpu.VMEM((H * tq, LANES), F32)],
        ),
        out_shape=jax.ShapeDtypeStruct((B, tq, aw), BF16),
        compiler_params=_cparams(("parallel", "arbitrary")),
        name="sample_attention",
    )(page_table, scores, iq_rows, wb, ik_new, q, k_new, v_new, z, *([cache_k] * G), *([cache_v] * G))


def _outproj_kernel(oa_ref, ob_ref, ma_ref, mb_ref, x_ref, gate_ref, wpa_ref, wpb_ref, wo_ref, gf_ref, o_ref, *, final):
    pa = _dot(oa_ref[0], wpa_ref[...])
    pb = _dot(ob_ref[0], wpb_ref[...])
    merged = _sigmoid(ma_ref[0]) * pa + _sigmoid(mb_ref[0]) * pb
    x = x_ref[0] + gate_ref[0] * _dot(merged.astype(BF16), wo_ref[...])
    if final:
        x = x * lax.rsqrt(jnp.mean(x * x, axis=-1, keepdims=True) + EPS) * gf_ref[...]
    o_ref[0] = x


def _outproj(o_a, o_b, p1, col1, x, gate, wpa, wpb, wo, g_final, final, tm):
    B, T, D = x.shape
    R = gate.shape[1]
    rb = 1 if R == 1 else tm
    gate_map = (lambda b, i: (b, 0, 0)) if R == 1 else (lambda b, i: (b, i, 0))
    const = lambda b, i: (0, 0)
    once = pl.Buffered(1)
    return pl.pallas_call(
        functools.partial(_outproj_kernel, final=final),
        grid=(B, T // tm),
        in_specs=[pl.BlockSpec((1, tm, o_a.shape[2]), lambda b, i: (b, i, 0)),
                  pl.BlockSpec((1, tm, o_b.shape[2]), lambda b, i: (b, i, 0)),
                  pl.BlockSpec((1, tm, D), lambda b, i: (b, i, col1["m_a"] // D)),
                  pl.BlockSpec((1, tm, D), lambda b, i: (b, i, col1["m_b"] // D)),
                  pl.BlockSpec((1, tm, D), lambda b, i: (b, i, 0)),
                  pl.BlockSpec((1, rb, D), gate_map),
                  pl.BlockSpec(wpa.shape, const, pipeline_mode=once),
                  pl.BlockSpec(wpb.shape, const, pipeline_mode=once),
                  pl.BlockSpec(wo.shape, const, pipeline_mode=once),
                  pl.BlockSpec((1, D), const)],
        out_specs=pl.BlockSpec((1, tm, D), lambda b, i: (b, i, 0)),
        out_shape=jax.ShapeDtypeStruct((B, T, D), F32),
        compiler_params=_cparams(("parallel", "parallel")),
        name="merge_outproj",
    )(o_a, o_b, p1, p1, x, gate, wpa, wpb, wo, g_final.reshape(1, D))


def _column_groups(d):
    gqk, gv, att, iqw = d // 4, d // 2, d // 2, d
    sizes = [("gq", gqk), ("gk", gqk), ("gv", gv), ("gz", gv), ("ga", GLA_RANK), ("aq", att), ("ak", att),
             ("av", att), ("az", att), ("iq", iqw), ("ik", LANES), ("iw", IDX_HEADS), ("m_a", d), ("m_b", d)]
    src, o = {}, 0
    for n, s in sizes:
        src[n] = (o, o + s)
        o += s
    groups = (("gq", "gk", "gv", "gz", "az", "m_a", "m_b"), ("ak", "av", "ik", "gaiw"), ("iq", "aq"))
    return src, groups


def _regroup_weights(w, src, names):
    parts, offs, o = [], {}, 0
    for n in names:
        if n == "gaiw":
            pad = LANES - GLA_RANK - IDX_HEADS
            part = jnp.concatenate([w[:, src["ga"][0]:src["ga"][1]], w[:, src["iw"][0]:src["iw"][1]],
                                    jnp.zeros((w.shape[0], pad), w.dtype)], axis=1)
        else:
            part = w[:, src[n][0]:src[n][1]]
        offs[n] = o
        o += part.shape[1]
        parts.append(part)
    return jnp.concatenate(parts, axis=1).astype(BF16), offs


def kernel(x_prompt, x_sample, cache_k, cache_v, cache_kidx, state_gla, page_table, c_prompt, c_sample, w_c, b_c,
           g_norm, w_in, w_a2, b_a2, g_gla, w_pa, w_pb, w_out, g_final):
    depth = w_in.shape[0]
    B, T, D = x_prompt.shape
    Bs, Ts, _ = x_sample.shape
    H, dk, dv = state_gla.shape[2], state_gla.shape[3], state_gla.shape[4]
    n_pages = page_table.shape[1]
    past = n_pages * PAGE_SIZE
    aw = ATT_HEADS * LANES
    src, groups = _column_groups(D)

    rows = -(-(B + Bs) // SUBLANES) * SUBLANES
    c_all = jnp.concatenate([c_prompt, c_sample, jnp.zeros((rows - B - Bs, D), F32)], axis=0)
    mod = _modulation(c_all, w_c, b_c)

    gla_consts = _gla_constants(GLA_CHUNK)
    score_pages = min(16, n_pages)
    attn_pages = min(8, n_pages)
    topk_p = min(TOPK_MAX, T // 4)
    topk_s = min(TOPK_MAX, (past + Ts) // 4)
    ts_pad = GLA_CHUNK

    xp, xs = x_prompt, x_sample.reshape(1, Bs * Ts, D)
    outs = {n: [] for n in ("kp", "vp", "ikp", "sp", "ks", "vs", "iks", "ss")}
    for l in range(depth):
        ws = [_regroup_weights(w_in[l], src, names) for names in groups]
        (w1, col1), (w2, col2), (w3, col3) = ws
        wa = jnp.concatenate([w_a2[l], jnp.zeros((LANES - GLA_RANK, w_a2.shape[2]), F32)], axis=0).astype(BF16)
        ba = b_a2[l].reshape(1, -1)
        wpa, wpb, wo = w_pa[l].astype(BF16), w_pb[l].astype(BF16), w_out[l].astype(BF16)
        final = l == depth - 1
        shift, scale, gate = (mod[l, :, i * D:(i + 1) * D] for i in range(3))

        sc, sh, gt = (a[:B].reshape(B, 1, D) for a in (scale, shift, gate))
        tm = min(T, 1024)
        (p1,) = _inproj(xp, sc, sh, g_norm[l], w1, (F32,), tm, 1024, "inproj_gates")
        p2f, p2b = _inproj(xp, sc, sh, g_norm[l], w2, (F32, BF16), tm, 1152, "inproj_kv")
        (p3,) = _inproj(xp, sc, sh, g_norm[l], w3, (BF16,), tm, 1024, "inproj_queries")
        s0 = jnp.zeros((B, H, dk, dv), F32)
        o_a, s_new = _gla(p1, p2f, {**col1, **col2}, s0, wa, ba, g_gla[l], gla_consts, 128, None)
        o_b = _dsa_prompt(p1, p2f, p2b, p3, col1, col2, col3, topk_p, 128)
        xp = _outproj(o_a, o_b, p1, col1, xp, gt, wpa, wpb, wo, g_final, final, 256)
        outs["kp"].append(p2f[:, :, col2["ak"]:col2["ak"] + aw].reshape(B, T, ATT_HEADS, LANES))
        outs["vp"].append(p2f[:, :, col2["av"]:col2["av"] + aw].reshape(B, T, ATT_HEADS, LANES))
        outs["ikp"].append(p2f[:, :, col2["ik"]:col2["ik"] + LANES])
        outs["sp"].append(s_new)

        n_s = Bs * Ts
        sc, sh, gt = (jnp.repeat(a[B:B + Bs], Ts, axis=0).reshape(1, n_s, D) for a in (scale, shift, gate))
        (q1,) = _inproj(xs, sc, sh, g_norm[l], w1, (F32,), n_s, 1024, "inproj_gates_s")
        (q2,) = _inproj(xs, sc, sh, g_norm[l], w2, (F32,), n_s, 1152, "inproj_kv_s")
        (q3,) = _inproj(xs, sc, sh, g_norm[l], w3, (BF16,), n_s, 1024, "inproj_queries_s")
        pad_t = lambda a: jnp.pad(a.reshape(Bs, Ts, -1), ((0, 0), (0, ts_pad - Ts), (0, 0)))
        o_a_s, s_new_s = _gla(pad_t(q1[0, :, :col1["az"]]), pad_t(q2[0]), {**col1, **col2}, state_gla[l], wa, ba,
                              g_gla[l], gla_consts, ts_pad, Ts)
        o_a_s = o_a_s[:, :Ts].reshape(1, n_s, -1)

        iq = q3[0, :, col3["iq"]:col3["iq"] + IDX_HEADS * LANES].reshape(Bs, Ts, IDX_HEADS, LANES)
        iq_rows = jnp.transpose(iq, (0, 2, 1, 3)).reshape(Bs, IDX_HEADS * Ts, LANES)
        iw = q2[0, :, col2["gaiw"] + GLA_RANK:col2["gaiw"] + GLA_RANK + IDX_HEADS].reshape(Bs, Ts, IDX_HEADS)
        wb = jnp.broadcast_to(jnp.transpose(iw, (0, 2, 1)).reshape(Bs, IDX_HEADS * Ts, 1), (Bs, IDX_HEADS * Ts, LANES))
        ik_new = q2[0, :, col2["ik"]:col2["ik"] + LANES].reshape(Bs, Ts, LANES)
        k_new = q2[0, :, col2["ak"]:col2["ak"] + aw].reshape(Bs, Ts, aw)
        v_new = q2[0, :, col2["av"]:col2["av"] + aw].reshape(Bs, Ts, aw)
        aq = q3[0, :, col3["aq"]:col3["aq"] + aw].reshape(Bs, Ts, aw)
        az = q1[0, :, col1["az"]:col1["az"] + aw].reshape(Bs, Ts, aw)
        scores = _sample_scores(page_table, iq_rows, wb, cache_kidx, l, score_pages)
        o_b_s = _sample_attn(page_table, scores, iq_rows, wb, ik_new, aq, k_new, v_new, az, cache_k, cache_v,
                             l, topk_s, attn_pages)
        xs = _outproj(o_a_s, o_b_s.reshape(1, n_s, aw), q1, col1, xs, gt, wpa, wpb, wo, g_final, final, n_s)
        outs["ks"].append(k_new.reshape(Bs, Ts, ATT_HEADS, LANES))
        outs["vs"].append(v_new.reshape(Bs, Ts, ATT_HEADS, LANES))
        outs["iks"].append(ik_new)
        outs["ss"].append(s_new_s)

    st = lambda n: jnp.stack(outs[n])
    return (xp, xs.reshape(Bs, Ts, D), st("kp"), st("vp"), st("ikp"), st("sp"),
            st("ks"), st("vs"), st("iks"), st("ss"))
```

```python
import functools

import numpy as np
import jax
import jax.numpy as jnp
from jax import lax
from jax.experimental import pallas as pl
from jax.experimental.pallas import tpu as pltpu

F32 = jnp.float32
BF16 = jnp.bfloat16
I32 = jnp.int32

EPS = 1e-6
PAGE_SIZE = 128
GLA_HEADS = 4
GLA_RANK = 16
GLA_TAU = 16.0
GLA_CHUNK = 64
ATT_HEADS = 8
IDX_HEADS = 16
TOPK_MAX = 256

LANES = 128
SUBLANES = 8
VMEM_LIMIT_BYTES = 56 * 1024 * 1024

NEG_BIG = -1e30
INT_MIN = -2147483648
KEY_NEG_INF = -2139095041


def _cparams(sem):
    return pltpu.CompilerParams(dimension_semantics=sem, vmem_limit_bytes=VMEM_LIMIT_BYTES)


def _dot(a, b):
    return jnp.dot(a, b, preferred_element_type=F32)


def _dot_nt(a, b):
    return lax.dot_general(a, b, (((1,), (1,)), ((), ())), preferred_element_type=F32)


def _sigmoid(x):
    return 1.0 / (1.0 + jnp.exp(-x))


def _sort_key(x):
    bits = pltpu.bitcast(x + 0.0, I32)
    return jnp.where(bits >= 0, bits, bits ^ jnp.int32(0x7FFFFFFF))


def _mod_kernel(c_ref, w_ref, b_ref, o_ref):
    o_ref[0] = _dot(c_ref[...].astype(BF16), w_ref[0].astype(BF16)) + b_ref[0]


def _modulation(c_all, w_c, b_c):
    depth, d, n = w_c.shape
    rows = c_all.shape[0]
    tn = 768
    return pl.pallas_call(
        _mod_kernel,
        grid=(depth, n // tn),
        in_specs=[pl.BlockSpec((rows, d), lambda l, j: (0, 0)),
                  pl.BlockSpec((1, d, tn), lambda l, j: (l, 0, j)),
                  pl.BlockSpec((1, 1, tn), lambda l, j: (l, 0, j))],
        out_specs=pl.BlockSpec((1, rows, tn), lambda l, j: (l, 0, j)),
        out_shape=jax.ShapeDtypeStruct((depth, rows, n), F32),
        compiler_params=_cparams(("parallel", "parallel")),
        name="adaln_modulation",
    )(c_all, w_c, b_c.reshape(depth, 1, n))


def _adaln_norm(x, g, scale, shift):
    y = x * lax.rsqrt(jnp.mean(x * x, axis=-1, keepdims=True) + EPS) * g
    return y * (1.0 + scale) + shift


def _row_spec(a, tm, ngrid):
    rb = 1 if a.shape[1] == 1 else tm
    if ngrid == 2:
        return pl.BlockSpec((1, rb, a.shape[2]), (lambda b, i: (b, 0, 0)) if rb == 1 else (lambda b, i: (b, i, 0)))
    return pl.BlockSpec((1, rb, a.shape[2]), (lambda b, i, j: (b, 0, 0)) if rb == 1 else (lambda b, i, j: (b, i, 0)))


def _prenorm_kernel(x_ref, sc_ref, sh_ref, g_ref, o_ref):
    o_ref[0] = _adaln_norm(x_ref[0], g_ref[...], sc_ref[0], sh_ref[0]).astype(o_ref.dtype)


def _prenorm(x, scale, shift, g, tm):
    B, T, D = x.shape
    return pl.pallas_call(
        _prenorm_kernel,
        grid=(B, T // tm),
        in_specs=[pl.BlockSpec((1, tm, D), lambda b, i: (b, i, 0)), _row_spec(scale, tm, 2), _row_spec(shift, tm, 2),
                  pl.BlockSpec((1, D), lambda b, i: (0, 0))],
        out_specs=pl.BlockSpec((1, tm, D), lambda b, i: (b, i, 0)),
        out_shape=jax.ShapeDtypeStruct((B, T, D), BF16),
        compiler_params=_cparams(("parallel", "parallel")),
        name="prenorm",
    )(x, scale, shift, g.reshape(1, D))


def _inproj_kernel(h_ref, w_ref, o_ref):
    o_ref[0] = _dot(h_ref[0], w_ref[...]).astype(o_ref.dtype)


def _inproj(h, w_all, layer, out_dtype, tm, tn, name):
    B, T, D = h.shape
    N = w_all.shape[2]
    return pl.pallas_call(
        _inproj_kernel,
        grid=(B, T // tm, N // tn),
        in_specs=[pl.BlockSpec((1, tm, D), lambda b, i, j: (b, i, 0)),
                  pl.BlockSpec((None, D, tn), lambda b, i, j: (layer, 0, j))],
        out_specs=pl.BlockSpec((1, tm, tn), lambda b, i, j: (b, i, j)),
        out_shape=jax.ShapeDtypeStruct((B, T, N), out_dtype),
        compiler_params=_cparams(("parallel", "parallel", "parallel")),
        name=name,
    )(h, w_all)


def _inproj_kv_kernel(h_ref, w_ref, *refs, aw, n_alias):
    k32, ik32, v32, ga32, kb, ikb, vb = refs[n_alias:]
    r = _dot(h_ref[0], w_ref[...])
    j = pl.program_id(2)

    @pl.when(j == 0)
    def _():
        k32[0] = r[:, :aw]
        ik32[0] = r[:, aw:]
        kb[0] = r[:, :aw].astype(BF16)
        ikb[0] = r[:, aw:].astype(BF16)

    @pl.when(j == 1)
    def _():
        v32[0] = r[:, :aw]
        ga32[0] = r[:, aw:]
        vb[0] = r[:, :aw].astype(BF16)


def _inproj_kv(h, w_all, layer, stacked, depth_out, layer_out, tm):
    B, T, D = h.shape
    aw = ATT_HEADS * LANES
    tn = aw + LANES
    stacked = () if stacked is None else tuple(stacked)
    any_spec = pl.BlockSpec(memory_space=pl.ANY)
    lay = lambda w: pl.BlockSpec((None, 1, tm, w), lambda b, i, j: (layer_out, b, i, 0))
    per = lambda w: pl.BlockSpec((1, tm, w), lambda b, i, j: (b, i, 0))
    f32s = lambda w: jax.ShapeDtypeStruct((depth_out, B, T, w), F32)
    outs = pl.pallas_call(
        functools.partial(_inproj_kv_kernel, aw=aw, n_alias=len(stacked)),
        grid=(B, T // tm, 2),
        in_specs=[pl.BlockSpec((1, tm, D), lambda b, i, j: (b, i, 0)),
                  pl.BlockSpec((None, D, tn), lambda b, i, j: (layer, 0, j))] + [any_spec] * len(stacked),
        out_specs=[lay(aw), lay(LANES), lay(aw), per(LANES), per(aw), per(LANES), per(aw)],
        out_shape=[f32s(aw), f32s(LANES), f32s(aw), jax.ShapeDtypeStruct((B, T, LANES), F32),
                   jax.ShapeDtypeStruct((B, T, aw), BF16), jax.ShapeDtypeStruct((B, T, LANES), BF16),
                   jax.ShapeDtypeStruct((B, T, aw), BF16)],
        input_output_aliases={2: 0, 3: 2, 4: 1} if stacked else {},
        compiler_params=_cparams(("parallel", "parallel", "arbitrary")),
        name="inproj_kv",
    )(h, w_all, *stacked)
    k32, ik32, v32, ga32, kb, ikb, vb = outs
    return (k32, v32, ik32), ga32, kb, ikb, vb


def _gla_constants(C):
    r = np.arange(C)[:, None]
    j = np.arange(C)[None, :]
    blocks = [(j <= r), (j > r)]
    masks = []
    w = C // 2
    while w >= 1:
        mid = (r // (2 * w)) * (2 * w) + w
        upper = r >= mid
        blocks.append(np.where(upper, (j >= mid) & (j <= r), (j > r) & (j < mid)))
        t, s = r, j
        same = (t // (2 * w)) == (s // (2 * w))
        t_up = t >= (t // (2 * w)) * (2 * w) + w
        s_lo = s < (s // (2 * w)) * (2 * w) + w
        masks.append(same & t_up & s_lo)
        w //= 2
    masks.append(r == j)
    W = np.concatenate(blocks, axis=0).astype(np.float32)
    M = np.stack(masks).astype(np.float32)
    return jnp.asarray(W, dtype=BF16), jnp.asarray(M, dtype=F32)


def _gla_kernel(q_ref, k_ref, v_ref, z_ref, ga_ref, wa_ref, ba_ref, gg_ref, s0_ref, W_ref, M_ref,
                o_ref, sout_ref, S_ref, *, C, nchunk, nlev, valid, H, dk, dv):
    ci = pl.program_id(1)

    @pl.when(ci == 0)
    def _():
        S_ref[...] = s0_ref[0]

    W = W_ref[...]
    for c in range(nchunk):
        rows = pl.ds(c * C, C)
        zz = _dot(ga_ref[0, rows, :].astype(BF16), wa_ref[...]) + ba_ref[...]
        g = -(jnp.maximum(-zz, 0.0) + jnp.log1p(jnp.exp(-jnp.abs(zz)))) * (1.0 / GLA_TAU)
        if valid is not None:
            tok = ci * (nchunk * C) + c * C + lax.broadcasted_iota(I32, g.shape, 0)
            g = jnp.where(tok < valid, g, 0.0)
        g1 = g.astype(BF16)
        r1 = g - g1.astype(F32)
        g2 = r1.astype(BF16)
        g3 = (r1 - g2.astype(F32)).astype(BF16)
        E_all = jnp.exp(_dot(W, g1) + _dot(W, g2) + _dot(W, g3))
        for h in range(H):
            kc = slice(h * dk, (h + 1) * dk)
            vc = slice(h * dv, (h + 1) * dv)
            E = E_all[:, kc]
            qs = q_ref[0, rows, kc] * (dk ** -0.5)
            k = k_ref[0, rows, kc]
            v = v_ref[0, rows, vc].astype(BF16)
            e_b = E[0:C]
            e_rest = E[C:2 * C]
            A = jnp.where(M_ref[nlev] > 0, _dot_nt(qs.astype(BF16), k.astype(BF16)), 0.0)
            for l in range(nlev):
                e_l = E[(2 + l) * C:(3 + l) * C]
                A = A + jnp.where(M_ref[l] > 0, _dot_nt((qs * e_l).astype(BF16), (k * e_l).astype(BF16)), 0.0)
            S = S_ref[h]
            o = _dot(A.astype(BF16), v) + _dot((qs * e_b).astype(BF16), S.astype(BF16))
            k_dec = k * e_rest
            e_last = e_b.T[:, C - 1:C]
            S_ref[h] = e_last * S + _dot(k_dec.T.astype(BF16), v)
            y = o * lax.rsqrt(jnp.mean(o * o, axis=-1, keepdims=True) + EPS) * gg_ref[h:h + 1, :]
            zg = z_ref[0, rows, vc]
            o_ref[0, rows, vc] = (y * (zg * _sigmoid(zg))).astype(o_ref.dtype)

    @pl.when(ci == pl.num_programs(1) - 1)
    def _():
        sout_ref[0] = S_ref[...]


def _gla(p1, p2, col, s0, wa, ba, gg, consts, tc, valid):
    B, T, _ = p1.shape
    H, dk, dv = s0.shape[1], s0.shape[2], s0.shape[3]
    W, M = consts
    C = GLA_CHUNK
    nlev = M.shape[0] - 1
    qo, ko, vo, zo = (col[n] // (H * w) for n, w in (("gq", dk), ("gk", dk), ("gv", dv), ("gz", dv)))
    gao = col["gaiw"] // LANES
    kern = functools.partial(_gla_kernel, C=C, nchunk=tc // C, nlev=nlev, valid=valid, H=H, dk=dk, dv=dv)
    return pl.pallas_call(
        kern,
        grid=(B, T // tc),
        in_specs=[pl.BlockSpec((1, tc, H * dk), lambda b, i: (b, i, qo)),
                  pl.BlockSpec((1, tc, H * dk), lambda b, i: (b, i, ko)),
                  pl.BlockSpec((1, tc, H * dv), lambda b, i: (b, i, vo)),
                  pl.BlockSpec((1, tc, H * dv), lambda b, i: (b, i, zo)),
                  pl.BlockSpec((1, tc, LANES), lambda b, i: (b, i, gao)),
                  pl.BlockSpec((LANES, H * dk), lambda b, i: (0, 0)),
                  pl.BlockSpec((1, H * dk), lambda b, i: (0, 0)),
                  pl.BlockSpec((H, dv), lambda b, i: (0, 0)),
                  pl.BlockSpec((1, H, dk, dv), lambda b, i: (b, 0, 0, 0)),
                  pl.BlockSpec(W.shape, lambda b, i: (0, 0)),
                  pl.BlockSpec(M.shape, lambda b, i: (0, 0, 0))],
        out_specs=[pl.BlockSpec((1, tc, H * dv), lambda b, i: (b, i, 0)),
                   pl.BlockSpec((1, H, dk, dv), lambda b, i: (b, 0, 0, 0))],
        out_shape=[jax.ShapeDtypeStruct((B, T, H * dv), BF16),
                   jax.ShapeDtypeStruct((B, H, dk, dv), F32)],
        scratch_shapes=[pltpu.VMEM((H, dk, dv), F32)],
        compiler_params=_cparams(("parallel", "arbitrary")),
        name="gla_branch",
    )(p1, p1, p1, p1, p2, wa, ba, gg, s0, W, M)


def _kth_key(count_ge, kk, rows):
    def body(i, u):
        cand = u | (jnp.int32(1) << (31 - i))
        ok = count_ge(cand ^ jnp.int32(INT_MIN)) >= kk
        return jnp.where(ok, cand, u)
    u = lax.fori_loop(0, 32, body, jnp.zeros((rows, 1), I32))
    return u ^ jnp.int32(INT_MIN)


def _kth_key_groups(keys, kk):
    def body(i, us):
        bit = jnp.int32(1) << (31 - i)
        cands = [u | bit for u in us]
        oks = [_count(key >= (c ^ jnp.int32(INT_MIN))) >= kk for key, c in zip(keys, cands)]
        return tuple(jnp.where(ok, c, u) for ok, c, u in zip(oks, cands, us))
    us = lax.fori_loop(0, 32, body, tuple(jnp.zeros((k.shape[0], 1), I32) for k in keys), unroll=2)
    return jnp.concatenate([u ^ jnp.int32(INT_MIN) for u in us], axis=0)


def _tie_cutoff(count_eq_below, need, rows, nbits):
    def body(i, c):
        cand = c | (jnp.int32(1) << (nbits - 1 - i))
        ok = count_eq_below(cand) <= need
        return jnp.where(ok, cand, c)
    return lax.fori_loop(0, nbits, body, jnp.zeros((rows, 1), I32))


def _count(mask):
    return jnp.sum(jnp.where(mask, 1.0, 0.0), axis=1, keepdims=True)


def _dsa_block(iq_ref, ik_ref, iw_ref, q_ref, k_ref, v_ref, z_ref, o_ref, bias_ref, *, S, tq, topk, iw_off, scale,
               idx_scale):
    q0 = pl.program_id(1) * tq
    pos = q0 + lax.broadcasted_iota(I32, (tq, S), 0)
    kpos = lax.broadcasted_iota(I32, (tq, S), 1)

    if S <= topk:
        bias_ref[:, 0:S] = jnp.where(kpos <= pos, 0.0, NEG_BIG)
    else:
        iw = iw_ref[0]
        w = [iw[:, iw_off + h:iw_off + h + 1] * idx_scale for h in range(IDX_HEADS)]
        kt = min(S, 2 * LANES)
        tiles = []
        for c in range(0, S, kt):
            ik = ik_ref[0, c:min(c + kt, S), :]
            acc = jnp.zeros((tq, ik.shape[0]), F32)
            for h in range(IDX_HEADS):
                acc = acc + jnp.maximum(_dot_nt(iq_ref[0, :, h * LANES:(h + 1) * LANES], ik), 0.0) * w[h]
            tiles.append(acc)
        score = jnp.concatenate(tiles, axis=1)
        key = _sort_key(jnp.where(kpos <= pos, score, -jnp.inf))

        kk = float(topk)
        ng = 4
        thr = _kth_key_groups([key[i * tq // ng:(i + 1) * tq // ng] for i in range(ng)], kk)
        thr_sel = jnp.maximum(thr, KEY_NEG_INF + 1)
        bias_ref[:, 0:S] = jnp.where(key >= thr_sel, 0.0, NEG_BIG)
        tied = jnp.where(_count(key >= thr_sel) > kk, 1.0, 0.0)

        @pl.when(jnp.max(tied) > 0.0)
        def _():
            need = kk - _count(key > thr)
            eq_pos = jnp.where(key == thr, kpos, jnp.int32(2 ** 30))
            cut = _tie_cutoff(lambda c_: _count(eq_pos < c_), need, tq, int(S).bit_length())
            keep_eq = jnp.where(key == thr, jnp.where(kpos < cut, 0.0, NEG_BIG), NEG_BIG)
            bias_ref[:, 0:S] = jnp.where(key >= thr_sel, jnp.where(key > thr, 0.0, keep_eq), NEG_BIG)

    log2e = 1.4426950408889634
    krel = (lax.broadcasted_iota(I32, (1, S), 1) - q0).astype(F32)
    for h in range(ATT_HEADS):
        cols = slice(h * LANES, (h + 1) * LANES)
        qk = _dot_nt(q_ref[0, :, cols], k_ref[0, 0:S, cols])
        logits = qk * (scale * log2e) + krel * ((2.0 ** -(h + 1)) * log2e) + bias_ref[:, 0:S]
        m = jnp.max(logits, axis=1, keepdims=True)
        p = jnp.exp2(logits - m)
        l = jnp.sum(p, axis=1, keepdims=True)
        out = _dot(p.astype(BF16), v_ref[0, 0:S, cols]) / l
        zg = z_ref[0, :, cols]
        o_ref[0, :, cols] = (out * (zg * _sigmoid(zg))).astype(o_ref.dtype)


def _dsa_kernel(*refs, tq, T, groups, **kw):
    qi = pl.program_id(1)
    per = (T // tq) // groups
    for g in range(groups):
        pl.when(qi // per == g)(functools.partial(_dsa_block, *refs, S=(g + 1) * per * tq, tq=tq, **kw))


def _dsa_prompt(p1, p3, ga32, ikb, kb, vb, col1, col3, topk, tq):
    B, T, _ = p1.shape
    aw = ATT_HEADS * LANES
    groups = min(8, T // tq)
    kern = functools.partial(_dsa_kernel, tq=tq, T=T, groups=groups, topk=topk, iw_off=GLA_RANK, scale=LANES ** -0.5,
                             idx_scale=(LANES * IDX_HEADS) ** -0.5)
    iq_o = col3["iq"] // (IDX_HEADS * LANES)
    return pl.pallas_call(
        kern,
        grid=(B, T // tq),
        in_specs=[pl.BlockSpec((1, tq, IDX_HEADS * LANES), lambda b, i: (b, i, iq_o)),
                  pl.BlockSpec((1, T, LANES), lambda b, i: (b, 0, 0)),
                  pl.BlockSpec((1, tq, LANES), lambda b, i: (b, i, 0)),
                  pl.BlockSpec((1, tq, aw), lambda b, i: (b, i, col3["aq"] // aw)),
                  pl.BlockSpec((1, T, aw), lambda b, i: (b, 0, 0)),
                  pl.BlockSpec((1, T, aw), lambda b, i: (b, 0, 0)),
                  pl.BlockSpec((1, tq, aw), lambda b, i: (b, i, col1["az"] // aw))],
        out_specs=pl.BlockSpec((1, tq, aw), lambda b, i: (b, i, 0)),
        out_shape=jax.ShapeDtypeStruct((B, T, aw), BF16),
        scratch_shapes=[pltpu.VMEM((tq, T), F32)],
        compiler_params=_cparams(("parallel", "arbitrary")),
        name="dsa_prompt",
    )(p3, ikb, ga32, p3, kb, vb, p1)


def _page_specs(n, block, layer, pages_per_step):
    def spec(g):
        return pl.BlockSpec(block, lambda b, p, pt: (layer, pt[b, p * pages_per_step + g]) + (0,) * (len(block) - 2))
    return [spec(g) for g in range(n)]


def _sample_scores_kernel(pt_ref, iq_ref, wb_ref, *rest, idx_scale):
    kidx_refs, o_ref = rest[:-1], rest[-1]
    iq = iq_ref[0]
    w = wb_ref[0] * idx_scale
    tq = o_ref.shape[1]
    for g, kidx_ref in enumerate(kidx_refs):
        s = jnp.maximum(_dot_nt(iq, kidx_ref[...].astype(BF16)), 0.0) * w
        o_ref[0, :, g * PAGE_SIZE:(g + 1) * PAGE_SIZE] = jnp.sum(s.reshape(IDX_HEADS, tq, PAGE_SIZE), axis=0)


def _sample_scores(page_table, iq_rows, wb, cache_kidx, layer, pages_per_step):
    B, n_pages = page_table.shape
    tq = iq_rows.shape[1] // IDX_HEADS
    G = pages_per_step
    kern = functools.partial(_sample_scores_kernel, idx_scale=(LANES * IDX_HEADS) ** -0.5)
    return pl.pallas_call(
        kern,
        grid_spec=pltpu.PrefetchScalarGridSpec(
            num_scalar_prefetch=1,
            grid=(B, n_pages // G),
            in_specs=[pl.BlockSpec((1, IDX_HEADS * tq, LANES), lambda b, p, pt: (b, 0, 0)),
                      pl.BlockSpec((1, IDX_HEADS * tq, LANES), lambda b, p, pt: (b, 0, 0))]
                     + _page_specs(G, (None, None, PAGE_SIZE, LANES), layer, G),
            out_specs=pl.BlockSpec((1, tq, G * PAGE_SIZE), lambda b, p, pt: (b, 0, p)),
        ),
        out_shape=jax.ShapeDtypeStruct((B, tq, n_pages * PAGE_SIZE), F32),
        compiler_params=_cparams(("parallel", "arbitrary")),
        name="sample_scores",
    )(page_table, iq_rows, wb, *([cache_kidx] * G))


def _sample_attn_kernel(pt_ref, sc_ref, iq_ref, wb_ref, ikn_ref, q_ref, kn_ref, vn_ref, z_ref, *rest,
                        G, tq, past, topk, scale, idx_scale):
    ck_refs, cv_refs, o_ref = rest[:G], rest[G:2 * G], rest[2 * G]
    qs_ref, thr_ref, cut_ref, bnew_ref, m_ref, l_ref, acc_ref = rest[2 * G + 1:]
    p = pl.program_id(1)
    H = ATT_HEADS
    lane = lax.broadcasted_iota(I32, (tq, LANES), 1)
    row = lax.broadcasted_iota(I32, (tq, LANES), 0)

    @pl.when(p == 0)
    def _():
        q = q_ref[0].astype(F32)
        qs_ref[...] = jnp.concatenate([q[:, h * LANES:(h + 1) * LANES] for h in range(H)], axis=0)
        ikn = jnp.concatenate([ikn_ref[0], jnp.zeros((LANES - tq, LANES), F32)], axis=0).astype(BF16)
        s = jnp.maximum(_dot_nt(iq_ref[0], ikn), 0.0) * (wb_ref[0] * idx_scale)
        s_new = jnp.sum(s.reshape(IDX_HEADS, tq, LANES), axis=0)
        key_new = _sort_key(jnp.where((lane <= row) & (lane < tq), s_new, -jnp.inf))
        key_past = _sort_key(sc_ref[0])

        kk = float(topk)
        count_ge = lambda t: _count(key_past >= t) + _count(key_new >= t)
        thr = _kth_key(count_ge, kk, tq)
        thr_ref[...] = jnp.broadcast_to(thr, thr_ref.shape)
        cut_ref[...] = jnp.full(cut_ref.shape, past + LANES, I32)
        tied = jnp.where((count_ge(thr) > kk) & (thr > KEY_NEG_INF), 1.0, 0.0)

        @pl.when(jnp.max(tied) > 0.0)
        def _():
            need = kk - _count(key_past > thr) - _count(key_new > thr)
            idx_past = lax.broadcasted_iota(I32, key_past.shape, 1)
            eq_past = key_past == thr
            eq_new = key_new == thr
            cnt = lambda c_: _count(eq_past & (idx_past < c_)) + _count(eq_new & (lane + past < c_))
            c = _tie_cutoff(cnt, need, tq, int(past + LANES).bit_length())
            cut_ref[...] = jnp.broadcast_to(c, cut_ref.shape)

        sel_new = (key_new > thr) | ((key_new == thr) & (lane + past < cut_ref[...]))
        bnew_ref[...] = jnp.where(sel_new & (lane <= row) & (lane < tq), 0.0, NEG_BIG)
        m_ref[...] = jnp.full(m_ref.shape, -jnp.inf, F32)
        l_ref[...] = jnp.zeros(l_ref.shape, F32)
        acc_ref[...] = jnp.zeros(acc_ref.shape, F32)

    thr = thr_ref[...]
    cut = cut_ref[...]
    bias, dist = [], []
    for g in range(G):
        first = pl.multiple_of((p * G + g) * PAGE_SIZE, PAGE_SIZE)
        key_p = _sort_key(sc_ref[0, :, pl.ds(first, PAGE_SIZE)])
        kpos = lane + first
        keep_eq = jnp.where(key_p == thr, jnp.where(kpos < cut, 0.0, NEG_BIG), NEG_BIG)
        bias.append(jnp.where(key_p > thr, 0.0, keep_eq))
        dist.append((row + past - kpos).astype(F32))
    head_rows = [pl.ds(h, PAGE_SIZE, stride=H) for h in range(H)]
    qs = qs_ref[...]
    logits = []
    for h in range(H):
        q_h = qs[h * tq:(h + 1) * tq].astype(BF16)
        logits.append([_dot_nt(q_h, ck_refs[g][head_rows[h], :].astype(BF16)) * scale
                       - (2.0 ** -(h + 1)) * dist[g] + bias[g] for g in range(G)])
    m_old = m_ref[:, 0:1]
    page_max = jnp.concatenate([functools.reduce(jnp.maximum, lg) for lg in logits], axis=0)
    m_new = jnp.maximum(m_old, jnp.max(page_max, axis=1, keepdims=True))
    alpha = jnp.exp(m_old - m_new)
    pe = [[jnp.exp(logits[h][g] - m_new[h * tq:(h + 1) * tq]) for g in range(G)] for h in range(H)]
    psum = jnp.concatenate([functools.reduce(jnp.add, ph) for ph in pe], axis=0)
    pv = [functools.reduce(jnp.add, [_dot(pe[h][g].astype(BF16), cv_refs[g][head_rows[h], :].astype(BF16))
                                     for g in range(G)]) for h in range(H)]
    l_new = alpha * l_ref[:, 0:1] + jnp.sum(psum, axis=1, keepdims=True)
    acc_ref[...] = alpha * acc_ref[...] + jnp.concatenate(pv, axis=0)
    l_ref[...] = jnp.broadcast_to(l_new, l_ref.shape)
    m_ref[...] = jnp.broadcast_to(m_new, m_ref.shape)

    @pl.when(p == pl.num_programs(1) - 1)
    def _():
        zpad = jnp.zeros((LANES - tq, LANES), F32)
        bnew = bnew_ref[...]
        dist = (row - lane).astype(F32)
        qs = qs_ref[...].astype(BF16)
        lg = []
        for h in range(H):
            cols = slice(h * LANES, (h + 1) * LANES)
            kn = jnp.concatenate([kn_ref[0, :, cols], zpad], axis=0).astype(BF16)
            qk = _dot_nt(qs, kn)[h * tq:(h + 1) * tq]
            lg.append(qk * scale - (2.0 ** -(h + 1)) * dist + bnew)
        logits_new = jnp.concatenate(lg, axis=0)
        m_old = m_ref[:, 0:1]
        m_new = jnp.maximum(m_old, jnp.max(logits_new, axis=1, keepdims=True))
        alpha = jnp.exp(m_old - m_new)
        pe = jnp.exp(logits_new - m_new)
        l_fin = alpha * l_ref[:, 0:1] + jnp.sum(pe, axis=1, keepdims=True)
        acc = alpha * acc_ref[...]
        pe = pe.astype(BF16)
        for h in range(H):
            cols = slice(h * LANES, (h + 1) * LANES)
            vn = jnp.concatenate([vn_ref[0, :, cols], zpad], axis=0).astype(BF16)
            rows = slice(h * tq, (h + 1) * tq)
            out = (acc[rows] + _dot(pe, vn)[rows]) / l_fin[rows]
            zg = z_ref[0, :, cols]
            o_ref[0, :, cols] = (out * (zg * _sigmoid(zg))).astype(o_ref.dtype)


def _sample_attn(page_table, scores, iq_rows, wb, ik_new, q, k_new, v_new, z, cache_k, cache_v, layer, topk,
                 pages_per_step):
    B, n_pages = page_table.shape
    tq = q.shape[1]
    H = ATT_HEADS
    aw = H * LANES
    past = n_pages * PAGE_SIZE
    G = pages_per_step
    kern = functools.partial(_sample_attn_kernel, G=G, tq=tq, past=past, topk=topk, scale=LANES ** -0.5,
                             idx_scale=(LANES * IDX_HEADS) ** -0.5)
    per_b = lambda b, p, pt: (b, 0, 0)
    page_block = (None, None, PAGE_SIZE * H, LANES)
    cache_k, cache_v = (c.reshape(c.shape[0], c.shape[1], PAGE_SIZE * H, LANES) for c in (cache_k, cache_v))
    return pl.pallas_call(
        kern,
        grid_spec=pltpu.PrefetchScalarGridSpec(
            num_scalar_prefetch=1,
            grid=(B, n_pages // G),
            in_specs=[pl.BlockSpec((1, tq, past), per_b),
                      pl.BlockSpec((1, IDX_HEADS * tq, LANES), per_b),
                      pl.BlockSpec((1, IDX_HEADS * tq, LANES), per_b),
                      pl.BlockSpec((1, tq, LANES), per_b),
                      pl.BlockSpec((1, tq, aw), per_b),
                      pl.BlockSpec((1, tq, aw), per_b),
                      pl.BlockSpec((1, tq, aw), per_b),
                      pl.BlockSpec((1, tq, aw), per_b)]
                     + _page_specs(G, page_block, layer, G) + _page_specs(G, page_block, layer, G),
            out_specs=pl.BlockSpec((1, tq, aw), per_b),
            scratch_shapes=[pltpu.VMEM((H * tq, LANES), F32),
                            pltpu.VMEM((tq, LANES), I32),
                            pltpu.VMEM((tq, LANES), I32),
                            pltpu.VMEM((tq, LANES), F32),
                            pltpu.VMEM((H * tq, LANES), F32),
                            pltpu.VMEM((H * tq, LANES), F32),
                            pltpu.VMEM((H * tq, LANES), F32)],
        ),
        out_shape=jax.ShapeDtypeStruct((B, tq, aw), BF16),
        compiler_params=_cparams(("parallel", "arbitrary")),
        name="sample_attention",
    )(page_table, scores, iq_rows, wb, ik_new, q, k_new, v_new, z, *([cache_k] * G), *([cache_v] * G))


def _outproj_kernel(oa_ref, ob_ref, ma_ref, mb_ref, x_ref, gate_ref, wpa_ref, wpb_ref, wo_ref, g_ref, *rest, final):
    pa = _dot(oa_ref[0], wpa_ref[...])
    pb = _dot(ob_ref[0], wpb_ref[...])
    merged = _sigmoid(ma_ref[0]) * pa + _sigmoid(mb_ref[0]) * pb
    x = x_ref[0] + gate_ref[0] * _dot(merged.astype(BF16), wo_ref[...])
    if final:
        (o_ref,) = rest
        o_ref[0] = x * lax.rsqrt(jnp.mean(x * x, axis=-1, keepdims=True) + EPS) * g_ref[...]
    else:
        sc_ref, sh_ref, o_ref, h_ref = rest
        o_ref[0] = x
        h_ref[0] = _adaln_norm(x, g_ref[...], sc_ref[0], sh_ref[0]).astype(h_ref.dtype)


def _outproj(o_a, o_b, p1, col1, x, gate, w_pa, w_pb, w_out, layer, g, nxt, tm):
    B, T, D = x.shape
    final = nxt is None
    const = lambda b, i: (0, 0)
    wspec = lambda w: pl.BlockSpec((None,) + w.shape[1:], lambda b, i: (layer, 0, 0), pipeline_mode=pl.Buffered(1))
    xspec = pl.BlockSpec((1, tm, D), lambda b, i: (b, i, 0))
    extra = [] if final else [_row_spec(nxt[0], tm, 2), _row_spec(nxt[1], tm, 2)]
    return pl.pallas_call(
        functools.partial(_outproj_kernel, final=final),
        grid=(B, T // tm),
        in_specs=[pl.BlockSpec((1, tm, o_a.shape[2]), lambda b, i: (b, i, 0)),
                  pl.BlockSpec((1, tm, o_b.shape[2]), lambda b, i: (b, i, 0)),
                  pl.BlockSpec((1, tm, D), lambda b, i: (b, i, col1["m_a"] // D)),
                  pl.BlockSpec((1, tm, D), lambda b, i: (b, i, col1["m_b"] // D)),
                  xspec, _row_spec(gate, tm, 2), wspec(w_pa), wspec(w_pb), wspec(w_out),
                  pl.BlockSpec((1, D), const)] + extra,
        out_specs=xspec if final else [xspec, xspec],
        out_shape=(jax.ShapeDtypeStruct((B, T, D), F32) if final else
                   [jax.ShapeDtypeStruct((B, T, D), F32), jax.ShapeDtypeStruct((B, T, D), BF16)]),
        compiler_params=_cparams(("parallel", "parallel")),
        name="merge_outproj",
    )(o_a, o_b, p1, p1, x, gate, w_pa, w_pb, w_out, g.reshape(1, D), *([] if final else nxt))


def _column_groups(d):
    gqk, gv, att, iqw = d // 4, d // 2, d // 2, d
    sizes = [("gq", gqk), ("gk", gqk), ("gv", gv), ("gz", gv), ("ga", GLA_RANK), ("aq", att), ("ak", att),
             ("av", att), ("az", att), ("iq", iqw), ("ik", LANES), ("iw", IDX_HEADS), ("m_a", d), ("m_b", d)]
    src, o = {}, 0
    for n, s in sizes:
        src[n] = (o, o + s)
        o += s
    groups = (("gq", "gk", "gv", "gz", "az", "m_a", "m_b"), ("ak", "ik", "av", "gaiw"), ("iq", "aq"))
    return src, groups


def _regroup_weights(w, src, names):
    parts, offs, o = [], {}, 0
    for n in names:
        if n == "gaiw":
            pad = LANES - GLA_RANK - IDX_HEADS
            part = jnp.concatenate([w[:, :, src["ga"][0]:src["ga"][1]], w[:, :, src["iw"][0]:src["iw"][1]],
                                    jnp.zeros(w.shape[:2] + (pad,), w.dtype)], axis=2)
        else:
            part = w[:, :, src[n][0]:src[n][1]]
        offs[n] = o
        o += part.shape[2]
        parts.append(part)
    return jnp.concatenate(parts, axis=2).astype(BF16), offs


def kernel(x_prompt, x_sample, cache_k, cache_v, cache_kidx, state_gla, page_table, c_prompt, c_sample, w_c, b_c,
           g_norm, w_in, w_a2, b_a2, g_gla, w_pa, w_pb, w_out, g_final):
    depth = w_in.shape[0]
    B, T, D = x_prompt.shape
    Bs, Ts, _ = x_sample.shape
    H, dk, dv = state_gla.shape[2], state_gla.shape[3], state_gla.shape[4]
    n_pages = page_table.shape[1]
    past = n_pages * PAGE_SIZE
    aw = ATT_HEADS * LANES
    n_s = Bs * Ts
    src, groups = _column_groups(D)

    rows = -(-(B + Bs) // SUBLANES) * SUBLANES
    c_all = jnp.concatenate([c_prompt, c_sample, jnp.zeros((rows - B - Bs, D), F32)], axis=0)
    mod = _modulation(c_all, w_c, b_c)

    (w1, col1), (w2, _), (w3, col3) = (_regroup_weights(w_in, src, names) for names in groups)
    wa = jnp.concatenate([w_a2, jnp.zeros((depth, LANES - GLA_RANK, w_a2.shape[2]), F32)], axis=1).astype(BF16)
    wpa, wpb, wo = w_pa.astype(BF16), w_pb.astype(BF16), w_out.astype(BF16)
    gla_col = {**col1, "gaiw": 0}

    gla_consts = _gla_constants(GLA_CHUNK)
    score_pages = min(16, n_pages)
    attn_pages = min(8, n_pages)
    topk_p = min(TOPK_MAX, T // 4)
    topk_s = min(TOPK_MAX, (past + Ts) // 4)
    ts_pad = GLA_CHUNK
    tm = min(T, 512)

    def modulation(l):
        shift, scale, gate = (mod[l, :, i * D:(i + 1) * D] for i in range(3))
        prompt = tuple(a[:B].reshape(B, 1, D) for a in (scale, shift, gate))
        sample = tuple(jnp.repeat(a[B:B + Bs], Ts, axis=0).reshape(1, n_s, D) for a in (scale, shift, gate))
        return prompt, sample

    xp, xs = x_prompt, x_sample.reshape(1, n_s, D)
    mod_p, mod_s = modulation(0)
    hp = _prenorm(xp, mod_p[0], mod_p[1], g_norm[0], tm)
    hs = _prenorm(xs, mod_s[0], mod_s[1], g_norm[0], n_s)
    stacked_p = None
    outs = {n: [] for n in ("sp", "ks", "vs", "iks", "ss")}
    for l in range(depth):
        final = l == depth - 1
        nxt_p, nxt_s = (None, None) if final else modulation(l + 1)
        g_next = g_final if final else g_norm[l + 1]
        ba = b_a2[l].reshape(1, -1)

        p1 = _inproj(hp, w1, l, F32, tm, 2048, "inproj_gates")
        p3 = _inproj(hp, w3, l, BF16, tm, 1024, "inproj_queries")
        stacked_p, ga32, kb, ikb, vb = _inproj_kv(hp, w2, l, stacked_p, depth, l, tm)
        s0 = jnp.zeros((B, H, dk, dv), F32)
        o_a, s_new = _gla(p1, ga32, gla_col, s0, wa[l], ba, g_gla[l], gla_consts, 128, None)
        o_b = _dsa_prompt(p1, p3, ga32, ikb, kb, vb, col1, col3, topk_p, 128)
        res = _outproj(o_a, o_b, p1, col1, xp, mod_p[2], wpa, wpb, wo, l, g_next,
                       None if final else nxt_p[:2], 256)
        xp, hp = (res, None) if final else res
        outs["sp"].append(s_new)

        q1 = _inproj(hs, w1, l, F32, n_s, 2048, "inproj_gates_s")
        q3 = _inproj(hs, w3, l, BF16, n_s, 1024, "inproj_queries_s")
        (k_s, v_s, ik_s), ga_s, _, _, _ = _inproj_kv(hs, w2, l, None, 1, 0, n_s)
        pad_t = lambda a: jnp.pad(a.reshape(Bs, Ts, -1), ((0, 0), (0, ts_pad - Ts), (0, 0)))
        o_a_s, s_new_s = _gla(pad_t(q1[0, :, :col1["az"]]), pad_t(ga_s[0]), gla_col, state_gla[l], wa[l], ba,
                              g_gla[l], gla_consts, ts_pad, Ts)
        o_a_s = o_a_s[:, :Ts].reshape(1, n_s, -1)

        iq = q3[0, :, col3["iq"]:col3["iq"] + IDX_HEADS * LANES].reshape(Bs, Ts, IDX_HEADS, LANES)
        iq_rows = jnp.transpose(iq, (0, 2, 1, 3)).reshape(Bs, IDX_HEADS * Ts, LANES)
        iw = ga_s[0, :, GLA_RANK:GLA_RANK + IDX_HEADS].reshape(Bs, Ts, IDX_HEADS)
        wb = jnp.broadcast_to(jnp.transpose(iw, (0, 2, 1)).reshape(Bs, IDX_HEADS * Ts, 1), (Bs, IDX_HEADS * Ts, LANES))
        ik_new = ik_s.reshape(Bs, Ts, LANES)
        k_new = k_s.reshape(Bs, Ts, aw)
        v_new = v_s.reshape(Bs, Ts, aw)
        aq = q3[0, :, col3["aq"]:col3["aq"] + aw].reshape(Bs, Ts, aw)
        az = q1[0, :, col1["az"]:col1["az"] + aw].reshape(Bs, Ts, aw)
        scores = _sample_scores(page_table, iq_rows, wb, cache_kidx, l, score_pages)
        o_b_s = _sample_attn(page_table, scores, iq_rows, wb, ik_new, aq, k_new, v_new, az, cache_k, cache_v,
                             l, topk_s, attn_pages)
        res = _outproj(o_a_s, o_b_s.reshape(1, n_s, aw), q1, col1, xs, mod_s[2], wpa, wpb, wo, l, g_next,
                       None if final else nxt_s[:2], n_s)
        xs, hs = (res, None) if final else res
        outs["ks"].append(k_new.reshape(Bs, Ts, ATT_HEADS, LANES))
        outs["vs"].append(v_new.reshape(Bs, Ts, ATT_HEADS, LANES))
        outs["iks"].append(ik_new)
        outs["ss"].append(s_new_s)
        mod_p, mod_s = nxt_p, nxt_s

    k_p, v_p, ik_p = stacked_p
    st = lambda n: jnp.stack(outs[n])
    return (xp, xs.reshape(Bs, Ts, D), k_p.reshape(depth, B, T, ATT_HEADS, LANES),
            v_p.reshape(depth, B, T, ATT_HEADS, LANES), ik_p, st("sp"), st("ks"), st("vs"), st("iks"), st("ss"))
```

```python
import functools

import numpy as np
import jax
import jax.numpy as jnp
from jax import lax
from jax.experimental import pallas as pl
from jax.experimental.pallas import tpu as pltpu

F32 = jnp.float32
BF16 = jnp.bfloat16
I32 = jnp.int32

EPS = 1e-6
PAGE_SIZE = 128
GLA_HEADS = 4
GLA_RANK = 16
GLA_TAU = 16.0
GLA_CHUNK = 64
ATT_HEADS = 8
IDX_HEADS = 16
TOPK_MAX = 256

LANES = 128
SUBLANES = 8
VMEM_LIMIT_BYTES = 56 * 1024 * 1024

NEG_BIG = -1e30
INT_MIN = -2147483648
KEY_NEG_INF = -2139095041


def _cparams(sem):
    return pltpu.CompilerParams(dimension_semantics=sem, vmem_limit_bytes=VMEM_LIMIT_BYTES)


def _dot(a, b):
    return jnp.dot(a, b, preferred_element_type=F32)


def _dot_nt(a, b):
    return lax.dot_general(a, b, (((1,), (1,)), ((), ())), preferred_element_type=F32)


def _sigmoid(x):
    return 1.0 / (1.0 + jnp.exp(-x))


def _sort_key(x):
    bits = pltpu.bitcast(x + 0.0, I32)
    return jnp.where(bits >= 0, bits, bits ^ jnp.int32(0x7FFFFFFF))


def _mod_kernel(c_ref, w_ref, b_ref, o_ref):
    o_ref[0] = _dot(c_ref[...].astype(BF16), w_ref[0].astype(BF16)) + b_ref[0]


def _modulation(c_all, w_c, b_c):
    depth, d, n = w_c.shape
    rows = c_all.shape[0]
    tn = 768
    return pl.pallas_call(
        _mod_kernel,
        grid=(depth, n // tn),
        in_specs=[pl.BlockSpec((rows, d), lambda l, j: (0, 0)),
                  pl.BlockSpec((1, d, tn), lambda l, j: (l, 0, j)),
                  pl.BlockSpec((1, 1, tn), lambda l, j: (l, 0, j))],
        out_specs=pl.BlockSpec((1, rows, tn), lambda l, j: (l, 0, j)),
        out_shape=jax.ShapeDtypeStruct((depth, rows, n), F32),
        compiler_params=_cparams(("parallel", "parallel")),
        name="adaln_modulation",
    )(c_all, w_c, b_c.reshape(depth, 1, n))


def _adaln_norm(x, g, scale, shift):
    y = x * lax.rsqrt(jnp.mean(x * x, axis=-1, keepdims=True) + EPS) * g
    return y * (1.0 + scale) + shift


def _row_spec(a, tm, ngrid):
    rb = 1 if a.shape[1] == 1 else tm
    if ngrid == 2:
        return pl.BlockSpec((1, rb, a.shape[2]), (lambda b, i: (b, 0, 0)) if rb == 1 else (lambda b, i: (b, i, 0)))
    return pl.BlockSpec((1, rb, a.shape[2]), (lambda b, i, j: (b, 0, 0)) if rb == 1 else (lambda b, i, j: (b, i, 0)))


def _prenorm_kernel(x_ref, sc_ref, sh_ref, g_ref, o_ref):
    o_ref[0] = _adaln_norm(x_ref[0], g_ref[...], sc_ref[0], sh_ref[0]).astype(o_ref.dtype)


def _prenorm(x, scale, shift, g, tm):
    B, T, D = x.shape
    return pl.pallas_call(
        _prenorm_kernel,
        grid=(B, T // tm),
        in_specs=[pl.BlockSpec((1, tm, D), lambda b, i: (b, i, 0)), _row_spec(scale, tm, 2), _row_spec(shift, tm, 2),
                  pl.BlockSpec((1, D), lambda b, i: (0, 0))],
        out_specs=pl.BlockSpec((1, tm, D), lambda b, i: (b, i, 0)),
        out_shape=jax.ShapeDtypeStruct((B, T, D), BF16),
        compiler_params=_cparams(("parallel", "parallel")),
        name="prenorm",
    )(x, scale, shift, g.reshape(1, D))


def _inproj_kernel(h_ref, w_ref, o_ref):
    o_ref[0] = _dot(h_ref[0], w_ref[...]).astype(o_ref.dtype)


def _inproj(h, w_all, layer, out_dtype, tm, tn, name):
    B, T, D = h.shape
    N = w_all.shape[2]
    return pl.pallas_call(
        _inproj_kernel,
        grid=(B, T // tm, N // tn),
        in_specs=[pl.BlockSpec((1, tm, D), lambda b, i, j: (b, i, 0)),
                  pl.BlockSpec((None, D, tn), lambda b, i, j: (layer, 0, j))],
        out_specs=pl.BlockSpec((1, tm, tn), lambda b, i, j: (b, i, j)),
        out_shape=jax.ShapeDtypeStruct((B, T, N), out_dtype),
        compiler_params=_cparams(("parallel", "parallel", "parallel")),
        name=name,
    )(h, w_all)


def _inproj_kv_kernel(h_ref, w_ref, *refs, aw, n_alias):
    k32, ik32, v32, ga32, kb, ikb, vb = refs[n_alias:]
    r = _dot(h_ref[0], w_ref[...])
    j = pl.program_id(2)

    @pl.when(j == 0)
    def _():
        k32[0] = r[:, :aw]
        ik32[0] = r[:, aw:]
        kb[0] = r[:, :aw].astype(BF16)
        ikb[0] = r[:, aw:].astype(BF16)

    @pl.when(j == 1)
    def _():
        v32[0] = r[:, :aw]
        ga32[0] = r[:, aw:]
        vb[0] = r[:, :aw].astype(BF16)


def _inproj_kv(h, w_all, layer, stacked, depth_out, layer_out, tm):
    B, T, D = h.shape
    aw = ATT_HEADS * LANES
    tn = aw + LANES
    stacked = () if stacked is None else tuple(stacked)
    any_spec = pl.BlockSpec(memory_space=pl.ANY)
    lay = lambda w: pl.BlockSpec((None, 1, tm, w), lambda b, i, j: (layer_out, b, i, 0))
    per = lambda w: pl.BlockSpec((1, tm, w), lambda b, i, j: (b, i, 0))
    f32s = lambda w: jax.ShapeDtypeStruct((depth_out, B, T, w), F32)
    outs = pl.pallas_call(
        functools.partial(_inproj_kv_kernel, aw=aw, n_alias=len(stacked)),
        grid=(B, T // tm, 2),
        in_specs=[pl.BlockSpec((1, tm, D), lambda b, i, j: (b, i, 0)),
                  pl.BlockSpec((None, D, tn), lambda b, i, j: (layer, 0, j))] + [any_spec] * len(stacked),
        out_specs=[lay(aw), lay(LANES), lay(aw), per(LANES), per(aw), per(LANES), per(aw)],
        out_shape=[f32s(aw), f32s(LANES), f32s(aw), jax.ShapeDtypeStruct((B, T, LANES), F32),
                   jax.ShapeDtypeStruct((B, T, aw), BF16), jax.ShapeDtypeStruct((B, T, LANES), BF16),
                   jax.ShapeDtypeStruct((B, T, aw), BF16)],
        input_output_aliases={2: 0, 3: 2, 4: 1} if stacked else {},
        compiler_params=_cparams(("parallel", "parallel", "arbitrary")),
        name="inproj_kv",
    )(h, w_all, *stacked)
    k32, ik32, v32, ga32, kb, ikb, vb = outs
    return (k32, v32, ik32), ga32, kb, ikb, vb


def _gla_constants(C):
    r = np.arange(C)[:, None]
    j = np.arange(C)[None, :]
    blocks = [(j <= r), (j > r)]
    masks = []
    w = C // 2
    while w >= 1:
        mid = (r // (2 * w)) * (2 * w) + w
        upper = r >= mid
        blocks.append(np.where(upper, (j >= mid) & (j <= r), (j > r) & (j < mid)))
        t, s = r, j
        same = (t // (2 * w)) == (s // (2 * w))
        t_up = t >= (t // (2 * w)) * (2 * w) + w
        s_lo = s < (s // (2 * w)) * (2 * w) + w
        masks.append(same & t_up & s_lo)
        w //= 2
    masks.append(r == j)
    W = np.concatenate(blocks, axis=0).astype(np.float32)
    M = np.stack(masks).astype(np.float32)
    return jnp.asarray(W, dtype=BF16), jnp.asarray(M, dtype=F32)


def _gla_kernel(q_ref, k_ref, v_ref, z_ref, ga_ref, wa_ref, ba_ref, gg_ref, s0_ref, W_ref, M_ref,
                o_ref, sout_ref, S_ref, *, C, nchunk, nlev, valid, H, dk, dv):
    ci = pl.program_id(1)

    @pl.when(ci == 0)
    def _():
        S_ref[...] = s0_ref[0]

    W = W_ref[...]
    for c in range(nchunk):
        rows = pl.ds(c * C, C)
        zz = _dot(ga_ref[0, rows, :].astype(BF16), wa_ref[...]) + ba_ref[...]
        g = -(jnp.maximum(-zz, 0.0) + jnp.log1p(jnp.exp(-jnp.abs(zz)))) * (1.0 / GLA_TAU)
        if valid is not None:
            tok = ci * (nchunk * C) + c * C + lax.broadcasted_iota(I32, g.shape, 0)
            g = jnp.where(tok < valid, g, 0.0)
        g1 = g.astype(BF16)
        r1 = g - g1.astype(F32)
        g2 = r1.astype(BF16)
        g3 = (r1 - g2.astype(F32)).astype(BF16)
        E_all = jnp.exp(_dot(W, g1) + _dot(W, g2) + _dot(W, g3))
        for h in range(H):
            kc = slice(h * dk, (h + 1) * dk)
            vc = slice(h * dv, (h + 1) * dv)
            E = E_all[:, kc]
            qs = q_ref[0, rows, kc] * (dk ** -0.5)
            k = k_ref[0, rows, kc]
            v = v_ref[0, rows, vc].astype(BF16)
            e_b = E[0:C]
            e_rest = E[C:2 * C]
            A = jnp.where(M_ref[nlev] > 0, _dot_nt(qs.astype(BF16), k.astype(BF16)), 0.0)
            for l in range(nlev):
                e_l = E[(2 + l) * C:(3 + l) * C]
                A = A + jnp.where(M_ref[l] > 0, _dot_nt((qs * e_l).astype(BF16), (k * e_l).astype(BF16)), 0.0)
            S = S_ref[h]
            o = _dot(A.astype(BF16), v) + _dot((qs * e_b).astype(BF16), S.astype(BF16))
            k_dec = k * e_rest
            e_last = e_b.T[:, C - 1:C]
            S_ref[h] = e_last * S + _dot(k_dec.T.astype(BF16), v)
            y = o * lax.rsqrt(jnp.mean(o * o, axis=-1, keepdims=True) + EPS) * gg_ref[h:h + 1, :]
            zg = z_ref[0, rows, vc]
            o_ref[0, rows, vc] = (y * (zg * _sigmoid(zg))).astype(o_ref.dtype)

    @pl.when(ci == pl.num_programs(1) - 1)
    def _():
        sout_ref[0] = S_ref[...]


def _gla(p1, p2, col, s0, wa, ba, gg, consts, tc, valid):
    B, T, _ = p1.shape
    H, dk, dv = s0.shape[1], s0.shape[2], s0.shape[3]
    W, M = consts
    C = GLA_CHUNK
    nlev = M.shape[0] - 1
    qo, ko, vo, zo = (col[n] // (H * w) for n, w in (("gq", dk), ("gk", dk), ("gv", dv), ("gz", dv)))
    gao = col["gaiw"] // LANES
    kern = functools.partial(_gla_kernel, C=C, nchunk=tc // C, nlev=nlev, valid=valid, H=H, dk=dk, dv=dv)
    return pl.pallas_call(
        kern,
        grid=(B, T // tc),
        in_specs=[pl.BlockSpec((1, tc, H * dk), lambda b, i: (b, i, qo)),
                  pl.BlockSpec((1, tc, H * dk), lambda b, i: (b, i, ko)),
                  pl.BlockSpec((1, tc, H * dv), lambda b, i: (b, i, vo)),
                  pl.BlockSpec((1, tc, H * dv), lambda b, i: (b, i, zo)),
                  pl.BlockSpec((1, tc, LANES), lambda b, i: (b, i, gao)),
                  pl.BlockSpec((LANES, H * dk), lambda b, i: (0, 0)),
                  pl.BlockSpec((1, H * dk), lambda b, i: (0, 0)),
                  pl.BlockSpec((H, dv), lambda b, i: (0, 0)),
                  pl.BlockSpec((1, H, dk, dv), lambda b, i: (b, 0, 0, 0)),
                  pl.BlockSpec(W.shape, lambda b, i: (0, 0)),
                  pl.BlockSpec(M.shape, lambda b, i: (0, 0, 0))],
        out_specs=[pl.BlockSpec((1, tc, H * dv), lambda b, i: (b, i, 0)),
                   pl.BlockSpec((1, H, dk, dv), lambda b, i: (b, 0, 0, 0))],
        out_shape=[jax.ShapeDtypeStruct((B, T, H * dv), BF16),
                   jax.ShapeDtypeStruct((B, H, dk, dv), F32)],
        scratch_shapes=[pltpu.VMEM((H, dk, dv), F32)],
        compiler_params=_cparams(("parallel", "arbitrary")),
        name="gla_branch",
    )(p1, p1, p1, p1, p2, wa, ba, gg, s0, W, M)


def _kth_key(count_ge, kk, rows):
    def body(i, u):
        cand = u | (jnp.int32(1) << (31 - i))
        ok = count_ge(cand ^ jnp.int32(INT_MIN)) >= kk
        return jnp.where(ok, cand, u)
    u = lax.fori_loop(0, 32, body, jnp.zeros((rows, 1), I32))
    return u ^ jnp.int32(INT_MIN)


def _kth_key_groups(keys, kk):
    def body(i, us):
        bit = jnp.int32(1) << (31 - i)
        cands = [u | bit for u in us]
        oks = [_count(key >= (c ^ jnp.int32(INT_MIN))) >= kk for key, c in zip(keys, cands)]
        return tuple(jnp.where(ok, c, u) for ok, c, u in zip(oks, cands, us))
    us = lax.fori_loop(0, 32, body, tuple(jnp.zeros((k.shape[0], 1), I32) for k in keys), unroll=2)
    return jnp.concatenate([u ^ jnp.int32(INT_MIN) for u in us], axis=0)


def _tie_cutoff(count_eq_below, need, rows, nbits):
    def body(i, c):
        cand = c | (jnp.int32(1) << (nbits - 1 - i))
        ok = count_eq_below(cand) <= need
        return jnp.where(ok, cand, c)
    return lax.fori_loop(0, nbits, body, jnp.zeros((rows, 1), I32))


def _count(mask):
    return jnp.sum(jnp.where(mask, 1.0, 0.0), axis=1, keepdims=True)


def _dsa_block(slope_ref, iq_ref, ik_ref, iw_ref, q_ref, k_ref, v_ref, z_ref, o_ref, bias_ref, *, S, tq, topk, iw_off,
               scale, idx_scale):
    q0 = pl.program_id(1) * tq
    pos = q0 + lax.broadcasted_iota(I32, (tq, S), 0)
    kpos = lax.broadcasted_iota(I32, (tq, S), 1)
    head_cols = lambda h: pl.ds(pl.multiple_of(h * LANES, LANES), LANES)

    if S <= topk:
        bias_ref[:, 0:S] = jnp.where(kpos <= pos, 0.0, NEG_BIG)
    else:
        iw = iw_ref[0]
        ik = ik_ref[0, 0:S, :]
        score = jnp.zeros((tq, S), F32)
        for h in range(IDX_HEADS):
            s = _dot_nt(iq_ref[0, :, h * LANES:(h + 1) * LANES], ik)
            score = score + jnp.maximum(s, 0.0) * (iw[:, iw_off + h:iw_off + h + 1] * idx_scale)
        key = _sort_key(jnp.where(kpos <= pos, score, -jnp.inf))

        kk = float(topk)
        ng = 2
        thr = _kth_key_groups([key[i * tq // ng:(i + 1) * tq // ng] for i in range(ng)], kk)
        thr_sel = jnp.maximum(thr, KEY_NEG_INF + 1)
        bias_ref[:, 0:S] = jnp.where(key >= thr_sel, 0.0, NEG_BIG)
        tied = jnp.where(_count(key >= thr_sel) > kk, 1.0, 0.0)

        @pl.when(jnp.max(tied) > 0.0)
        def _():
            need = kk - _count(key > thr)
            eq_pos = jnp.where(key == thr, kpos, jnp.int32(2 ** 30))
            cut = _tie_cutoff(lambda c_: _count(eq_pos < c_), need, tq, int(S).bit_length())
            keep_eq = jnp.where(key == thr, jnp.where(kpos < cut, 0.0, NEG_BIG), NEG_BIG)
            bias_ref[:, 0:S] = jnp.where(key >= thr_sel, jnp.where(key > thr, 0.0, keep_eq), NEG_BIG)

    log2e = 1.4426950408889634
    krel = (lax.broadcasted_iota(I32, (1, S), 1) - q0).astype(F32)

    def attend(h, carry):
        cols = head_cols(h)
        qk = _dot_nt(q_ref[0, :, cols], k_ref[0, 0:S, cols])
        logits = qk * (scale * log2e) + krel * (slope_ref[h] * log2e) + bias_ref[:, 0:S]
        m = jnp.max(logits, axis=1, keepdims=True)
        p = jnp.exp2(logits - m)
        l = jnp.sum(p, axis=1, keepdims=True)
        out = _dot(p.astype(BF16), v_ref[0, 0:S, cols]) / l
        zg = z_ref[0, :, cols]
        o_ref[0, :, cols] = (out * (zg * _sigmoid(zg))).astype(o_ref.dtype)
        return carry

    lax.fori_loop(0, ATT_HEADS, attend, 0, unroll=4)


def _key_extents(T, tq, topk):
    marks = sorted({min(T, max(tq, topk)), T // 4, 3 * T // 8, T // 2, 3 * T // 4, T})
    return [m for m in marks if m % tq == 0 and m >= tq]


def _dsa_kernel(*refs, tq, T, extents, **kw):
    last = (pl.program_id(1) + 1) * tq
    lo = 0
    for S in extents:
        pl.when((last > lo) & (last <= S))(functools.partial(_dsa_block, *refs, S=S, tq=tq, **kw))
        lo = S


def _dsa_prompt(p1, p3, ga32, ikb, kb, vb, col1, col3, topk, tq):
    B, T, _ = p1.shape
    aw = ATT_HEADS * LANES
    kern = functools.partial(_dsa_kernel, tq=tq, T=T, extents=_key_extents(T, tq, topk), topk=topk, iw_off=GLA_RANK,
                             scale=LANES ** -0.5, idx_scale=(LANES * IDX_HEADS) ** -0.5)
    iq_o = col3["iq"] // (IDX_HEADS * LANES)
    slopes = jnp.asarray(2.0 ** -(np.arange(ATT_HEADS) + 1.0), F32)
    return pl.pallas_call(
        kern,
        grid=(B, T // tq),
        in_specs=[pl.BlockSpec(memory_space=pltpu.SMEM),
                  pl.BlockSpec((1, tq, IDX_HEADS * LANES), lambda b, i: (b, i, iq_o)),
                  pl.BlockSpec((1, T, LANES), lambda b, i: (b, 0, 0)),
                  pl.BlockSpec((1, tq, LANES), lambda b, i: (b, i, 0)),
                  pl.BlockSpec((1, tq, aw), lambda b, i: (b, i, col3["aq"] // aw)),
                  pl.BlockSpec((1, T, aw), lambda b, i: (b, 0, 0)),
                  pl.BlockSpec((1, T, aw), lambda b, i: (b, 0, 0)),
                  pl.BlockSpec((1, tq, aw), lambda b, i: (b, i, col1["az"] // aw))],
        out_specs=pl.BlockSpec((1, tq, aw), lambda b, i: (b, i, 0)),
        out_shape=jax.ShapeDtypeStruct((B, T, aw), BF16),
        scratch_shapes=[pltpu.VMEM((tq, T), F32)],
        compiler_params=_cparams(("parallel", "arbitrary")),
        name="dsa_prompt",
    )(slopes, p3, ikb, ga32, p3, kb, vb, p1)


def _page_specs(n, block, layer, pages_per_step):
    def spec(g):
        return pl.BlockSpec(block, lambda b, p, pt: (layer, pt[b, p * pages_per_step + g]) + (0,) * (len(block) - 2))
    return [spec(g) for g in range(n)]


def _sample_scores_kernel(pt_ref, iq_ref, wb_ref, *rest, idx_scale):
    kidx_refs, o_ref = rest[:-1], rest[-1]
    iq = iq_ref[0]
    w = wb_ref[0] * idx_scale
    tq = o_ref.shape[1]
    for g, kidx_ref in enumerate(kidx_refs):
        s = jnp.maximum(_dot_nt(iq, kidx_ref[...].astype(BF16)), 0.0) * w
        o_ref[0, :, g * PAGE_SIZE:(g + 1) * PAGE_SIZE] = jnp.sum(s.reshape(IDX_HEADS, tq, PAGE_SIZE), axis=0)


def _sample_scores(page_table, iq_rows, wb, cache_kidx, layer, pages_per_step):
    B, n_pages = page_table.shape
    tq = iq_rows.shape[1] // IDX_HEADS
    G = pages_per_step
    kern = functools.partial(_sample_scores_kernel, idx_scale=(LANES * IDX_HEADS) ** -0.5)
    return pl.pallas_call(
        kern,
        grid_spec=pltpu.PrefetchScalarGridSpec(
            num_scalar_prefetch=1,
            grid=(B, n_pages // G),
            in_specs=[pl.BlockSpec((1, IDX_HEADS * tq, LANES), lambda b, p, pt: (b, 0, 0)),
                      pl.BlockSpec((1, IDX_HEADS * tq, LANES), lambda b, p, pt: (b, 0, 0))]
                     + _page_specs(G, (None, None, PAGE_SIZE, LANES), layer, G),
            out_specs=pl.BlockSpec((1, tq, G * PAGE_SIZE), lambda b, p, pt: (b, 0, p)),
        ),
        out_shape=jax.ShapeDtypeStruct((B, tq, n_pages * PAGE_SIZE), F32),
        compiler_params=_cparams(("parallel", "arbitrary")),
        name="sample_scores",
    )(page_table, iq_rows, wb, *([cache_kidx] * G))


def _sample_attn_kernel(pt_ref, sc_ref, iq_ref, wb_ref, ikn_ref, q_ref, kn_ref, vn_ref, z_ref, *rest,
                        G, tq, past, topk, scale, idx_scale):
    ck_refs, cv_refs, o_ref = rest[:G], rest[G:2 * G], rest[2 * G]
    qs_ref, thr_ref, cut_ref, bnew_ref, m_ref, l_ref, acc_ref = rest[2 * G + 1:]
    p = pl.program_id(1)
    H = ATT_HEADS
    lane = lax.broadcasted_iota(I32, (tq, LANES), 1)
    row = lax.broadcasted_iota(I32, (tq, LANES), 0)

    @pl.when(p == 0)
    def _():
        q = q_ref[0].astype(F32)
        qs_ref[...] = jnp.concatenate([q[:, h * LANES:(h + 1) * LANES] for h in range(H)], axis=0)
        ikn = jnp.concatenate([ikn_ref[0], jnp.zeros((LANES - tq, LANES), F32)], axis=0).astype(BF16)
        s = jnp.maximum(_dot_nt(iq_ref[0], ikn), 0.0) * (wb_ref[0] * idx_scale)
        s_new = jnp.sum(s.reshape(IDX_HEADS, tq, LANES), axis=0)
        key_new = _sort_key(jnp.where((lane <= row) & (lane < tq), s_new, -jnp.inf))
        key_past = _sort_key(sc_ref[0])

        kk = float(topk)
        count_ge = lambda t: _count(key_past >= t) + _count(key_new >= t)
        thr = _kth_key(count_ge, kk, tq)
        thr_ref[...] = jnp.broadcast_to(thr, thr_ref.shape)
        cut_ref[...] = jnp.full(cut_ref.shape, past + LANES, I32)
        tied = jnp.where((count_ge(thr) > kk) & (thr > KEY_NEG_INF), 1.0, 0.0)

        @pl.when(jnp.max(tied) > 0.0)
        def _():
            need = kk - _count(key_past > thr) - _count(key_new > thr)
            idx_past = lax.broadcasted_iota(I32, key_past.shape, 1)
            eq_past = key_past == thr
            eq_new = key_new == thr
            cnt = lambda c_: _count(eq_past & (idx_past < c_)) + _count(eq_new & (lane + past < c_))
            c = _tie_cutoff(cnt, need, tq, int(past + LANES).bit_length())
            cut_ref[...] = jnp.broadcast_to(c, cut_ref.shape)

        sel_new = (key_new > thr) | ((key_new == thr) & (lane + past < cut_ref[...]))
        bnew_ref[...] = jnp.where(sel_new & (lane <= row) & (lane < tq), 0.0, NEG_BIG)
        m_ref[...] = jnp.full(m_ref.shape, -jnp.inf, F32)
        l_ref[...] = jnp.zeros(l_ref.shape, F32)
        acc_ref[...] = jnp.zeros(acc_ref.shape, F32)

    thr = thr_ref[...]
    cut = cut_ref[...]
    bias, dist = [], []
    for g in range(G):
        first = pl.multiple_of((p * G + g) * PAGE_SIZE, PAGE_SIZE)
        key_p = _sort_key(sc_ref[0, :, pl.ds(first, PAGE_SIZE)])
        kpos = lane + first
        keep_eq = jnp.where(key_p == thr, jnp.where(kpos < cut, 0.0, NEG_BIG), NEG_BIG)
        bias.append(jnp.where(key_p > thr, 0.0, keep_eq))
        dist.append((row + past - kpos).astype(F32))
    head_rows = [pl.ds(h, PAGE_SIZE, stride=H) for h in range(H)]
    qs = qs_ref[...]
    logits = []
    for h in range(H):
        q_h = qs[h * tq:(h + 1) * tq].astype(BF16)
        logits.append([_dot_nt(q_h, ck_refs[g][head_rows[h], :].astype(BF16)) * scale
                       - (2.0 ** -(h + 1)) * dist[g] + bias[g] for g in range(G)])
    m_old = m_ref[:, 0:1]
    page_max = jnp.concatenate([functools.reduce(jnp.maximum, lg) for lg in logits], axis=0)
    m_new = jnp.maximum(m_old, jnp.max(page_max, axis=1, keepdims=True))
    alpha = jnp.exp(m_old - m_new)
    pe = [[jnp.exp(logits[h][g] - m_new[h * tq:(h + 1) * tq]) for g in range(G)] for h in range(H)]
    psum = jnp.concatenate([functools.reduce(jnp.add, ph) for ph in pe], axis=0)
    pv = [functools.reduce(jnp.add, [_dot(pe[h][g].astype(BF16), cv_refs[g][head_rows[h], :].astype(BF16))
                                     for g in range(G)]) for h in range(H)]
    l_new = alpha * l_ref[:, 0:1] + jnp.sum(psum, axis=1, keepdims=True)
    acc_ref[...] = alpha * acc_ref[...] + jnp.concatenate(pv, axis=0)
    l_ref[...] = jnp.broadcast_to(l_new, l_ref.shape)
    m_ref[...] = jnp.broadcast_to(m_new, m_ref.shape)

    @pl.when(p == pl.num_programs(1) - 1)
    def _():
        zpad = jnp.zeros((LANES - tq, LANES), F32)
        bnew = bnew_ref[...]
        dist = (row - lane).astype(F32)
        qs = qs_ref[...].astype(BF16)
        lg = []
        for h in range(H):
            cols = slice(h * LANES, (h + 1) * LANES)
            kn = jnp.concatenate([kn_ref[0, :, cols], zpad], axis=0).astype(BF16)
            qk = _dot_nt(qs, kn)[h * tq:(h + 1) * tq]
            lg.append(qk * scale - (2.0 ** -(h + 1)) * dist + bnew)
        logits_new = jnp.concatenate(lg, axis=0)
        m_old = m_ref[:, 0:1]
        m_new = jnp.maximum(m_old, jnp.max(logits_new, axis=1, keepdims=True))
        alpha = jnp.exp(m_old - m_new)
        pe = jnp.exp(logits_new - m_new)
        l_fin = alpha * l_ref[:, 0:1] + jnp.sum(pe, axis=1, keepdims=True)
        acc = alpha * acc_ref[...]
        pe = pe.astype(BF16)
        for h in range(H):
            cols = slice(h * LANES, (h + 1) * LANES)
            vn = jnp.concatenate([vn_ref[0, :, cols], zpad], axis=0).astype(BF16)
            rows = slice(h * tq, (h + 1) * tq)
            out = (acc[rows] + _dot(pe, vn)[rows]) / l_fin[rows]
            zg = z_ref[0, :, cols]
            o_ref[0, :, cols] = (out * (zg * _sigmoid(zg))).astype(o_ref.dtype)


def _sample_attn(page_table, scores, iq_rows, wb, ik_new, q, k_new, v_new, z, cache_k, cache_v, layer, topk,
                 pages_per_step):
    B, n_pages = page_table.shape
    tq = q.shape[1]
    H = ATT_HEADS
    aw = H * LANES
    past = n_pages * PAGE_SIZE
    G = pages_per_step
    kern = functools.partial(_sample_attn_kernel, G=G, tq=tq, past=past, topk=topk, scale=LANES ** -0.5,
                             idx_scale=(LANES * IDX_HEADS) ** -0.5)
    per_b = lambda b, p, pt: (b, 0, 0)
    page_block = (None, None, PAGE_SIZE * H, LANES)
    cache_k, cache_v = (c.reshape(c.shape[0], c.shape[1], PAGE_SIZE * H, LANES) for c in (cache_k, cache_v))
    return pl.pallas_call(
        kern,
        grid_spec=pltpu.PrefetchScalarGridSpec(
            num_scalar_prefetch=1,
            grid=(B, n_pages // G),
            in_specs=[pl.BlockSpec((1, tq, past), per_b),
                      pl.BlockSpec((1, IDX_HEADS * tq, LANES), per_b),
                      pl.BlockSpec((1, IDX_HEADS * tq, LANES), per_b),
                      pl.BlockSpec((1, tq, LANES), per_b),
                      pl.BlockSpec((1, tq, aw), per_b),
                      pl.BlockSpec((1, tq, aw), per_b),
                      pl.BlockSpec((1, tq, aw), per_b),
                      pl.BlockSpec((1, tq, aw), per_b)]
                     + _page_specs(G, page_block, layer, G) + _page_specs(G, page_block, layer, G),
            out_specs=pl.BlockSpec((1, tq, aw), per_b),
            scratch_shapes=[pltpu.VMEM((H * tq, LANES), F32),
                            pltpu.VMEM((tq, LANES), I32),
                            pltpu.VMEM((tq, LANES), I32),
                            pltpu.VMEM((tq, LANES), F32),
                            pltpu.VMEM((H * tq, LANES), F32),
                            pltpu.VMEM((H * tq, LANES), F32),
                            pltpu.VMEM((H * tq, LANES), F32)],
        ),
        out_shape=jax.ShapeDtypeStruct((B, tq, aw), BF16),
        compiler_params=_cparams(("parallel", "arbitrary")),
        name="sample_attention",
    )(page_table, scores, iq_rows, wb, ik_new, q, k_new, v_new, z, *([cache_k] * G), *([cache_v] * G))


def _outproj_kernel(oa_ref, ob_ref, ma_ref, mb_ref, x_ref, gate_ref, wpa_ref, wpb_ref, wo_ref, g_ref, *rest, final):
    pa = _dot(oa_ref[0], wpa_ref[...])
    pb = _dot(ob_ref[0], wpb_ref[...])
    merged = _sigmoid(ma_ref[0]) * pa + _sigmoid(mb_ref[0]) * pb
    x = x_ref[0] + gate_ref[0] * _dot(merged.astype(BF16), wo_ref[...])
    if final:
        (o_ref,) = rest
        o_ref[0] = x * lax.rsqrt(jnp.mean(x * x, axis=-1, keepdims=True) + EPS) * g_ref[...]
    else:
        sc_ref, sh_ref, o_ref, h_ref = rest
        o_ref[0] = x
        h_ref[0] = _adaln_norm(x, g_ref[...], sc_ref[0], sh_ref[0]).astype(h_ref.dtype)


def _outproj(o_a, o_b, p1, col1, x, gate, w_pa, w_pb, w_out, layer, g, nxt, tm):
    B, T, D = x.shape
    final = nxt is None
    const = lambda b, i: (0, 0)
    wspec = lambda w: pl.BlockSpec((None,) + w.shape[1:], lambda b, i: (layer, 0, 0), pipeline_mode=pl.Buffered(1))
    xspec = pl.BlockSpec((1, tm, D), lambda b, i: (b, i, 0))
    extra = [] if final else [_row_spec(nxt[0], tm, 2), _row_spec(nxt[1], tm, 2)]
    return pl.pallas_call(
        functools.partial(_outproj_kernel, final=final),
        grid=(B, T // tm),
        in_specs=[pl.BlockSpec((1, tm, o_a.shape[2]), lambda b, i: (b, i, 0)),
                  pl.BlockSpec((1, tm, o_b.shape[2]), lambda b, i: (b, i, 0)),
                  pl.BlockSpec((1, tm, D), lambda b, i: (b, i, col1["m_a"] // D)),
                  pl.BlockSpec((1, tm, D), lambda b, i: (b, i, col1["m_b"] // D)),
                  xspec, _row_spec(gate, tm, 2), wspec(w_pa), wspec(w_pb), wspec(w_out),
                  pl.BlockSpec((1, D), const)] + extra,
        out_specs=xspec if final else [xspec, xspec],
        out_shape=(jax.ShapeDtypeStruct((B, T, D), F32) if final else
                   [jax.ShapeDtypeStruct((B, T, D), F32), jax.ShapeDtypeStruct((B, T, D), BF16)]),
        compiler_params=_cparams(("parallel", "parallel")),
        name="merge_outproj",
    )(o_a, o_b, p1, p1, x, gate, w_pa, w_pb, w_out, g.reshape(1, D), *([] if final else nxt))


def _column_groups(d):
    gqk, gv, att, iqw = d // 4, d // 2, d // 2, d
    sizes = [("gq", gqk), ("gk", gqk), ("gv", gv), ("gz", gv), ("ga", GLA_RANK), ("aq", att), ("ak", att),
             ("av", att), ("az", att), ("iq", iqw), ("ik", LANES), ("iw", IDX_HEADS), ("m_a", d), ("m_b", d)]
    src, o = {}, 0
    for n, s in sizes:
        src[n] = (o, o + s)
        o += s
    groups = (("gq", "gk", "gv", "gz", "az", "m_a", "m_b"), ("ak", "ik", "av", "gaiw"), ("iq", "aq"))
    return src, groups


def _group_layout(src, names):
    offs, o = {}, 0
    for n in names:
        offs[n] = o
        o += LANES if n == "gaiw" else src[n][1] - src[n][0]
    return offs, o


def _regroup_kernel(w_ref, *out_refs, src, groups):
    for names, o_ref in zip(groups, out_refs):
        offs, _ = _group_layout(src, names)
        for n in names:
            if n == "gaiw":
                o = offs[n]
                o_ref[:, o:o + GLA_RANK] = w_ref[:, src["ga"][0]:src["ga"][1]].astype(BF16)
                o_ref[:, o + GLA_RANK:o + GLA_RANK + IDX_HEADS] = w_ref[:, src["iw"][0]:src["iw"][1]].astype(BF16)
                pad = LANES - GLA_RANK - IDX_HEADS
                o_ref[:, o + LANES - pad:o + LANES] = jnp.zeros((o_ref.shape[0], pad), BF16)
            else:
                a, b = src[n]
                o_ref[:, offs[n]:offs[n] + b - a] = w_ref[:, a:b].astype(BF16)


def _regroup_weights(w, src, groups):
    depth, D, n_in = w.shape
    tk = 256
    widths = [_group_layout(src, names)[1] for names in groups]
    return pl.pallas_call(
        functools.partial(_regroup_kernel, src=src, groups=groups),
        grid=(depth, D // tk),
        in_specs=[pl.BlockSpec((None, tk, n_in), lambda l, i: (l, i, 0))],
        out_specs=[pl.BlockSpec((None, tk, n), lambda l, i: (l, i, 0)) for n in widths],
        out_shape=[jax.ShapeDtypeStruct((depth, D, n), BF16) for n in widths],
        compiler_params=_cparams(("parallel", "parallel")),
        name="regroup_weights",
    )(w)


def kernel(x_prompt, x_sample, cache_k, cache_v, cache_kidx, state_gla, page_table, c_prompt, c_sample, w_c, b_c,
           g_norm, w_in, w_a2, b_a2, g_gla, w_pa, w_pb, w_out, g_final):
    depth = w_in.shape[0]
    B, T, D = x_prompt.shape
    Bs, Ts, _ = x_sample.shape
    H, dk, dv = state_gla.shape[2], state_gla.shape[3], state_gla.shape[4]
    n_pages = page_table.shape[1]
    past = n_pages * PAGE_SIZE
    aw = ATT_HEADS * LANES
    n_s = Bs * Ts
    src, groups = _column_groups(D)

    rows = -(-(B + Bs) // SUBLANES) * SUBLANES
    c_all = jnp.concatenate([c_prompt, c_sample, jnp.zeros((rows - B - Bs, D), F32)], axis=0)
    mod = _modulation(c_all, w_c, b_c)

    w1, w2, w3 = _regroup_weights(w_in, src, groups)
    col1, col3 = _group_layout(src, groups[0])[0], _group_layout(src, groups[2])[0]
    wa = jnp.concatenate([w_a2, jnp.zeros((depth, LANES - GLA_RANK, w_a2.shape[2]), F32)], axis=1).astype(BF16)
    wpa, wpb, wo = w_pa.astype(BF16), w_pb.astype(BF16), w_out.astype(BF16)
    gla_col = {**col1, "gaiw": 0}

    gla_consts = _gla_constants(GLA_CHUNK)
    score_pages = min(16, n_pages)
    attn_pages = min(8, n_pages)
    topk_p = min(TOPK_MAX, T // 4)
    topk_s = min(TOPK_MAX, (past + Ts) // 4)
    ts_pad = GLA_CHUNK
    tm = min(T, 512)

    def modulation(l):
        shift, scale, gate = (mod[l, :, i * D:(i + 1) * D] for i in range(3))
        prompt = tuple(a[:B].reshape(B, 1, D) for a in (scale, shift, gate))
        sample = tuple(jnp.repeat(a[B:B + Bs], Ts, axis=0).reshape(1, n_s, D) for a in (scale, shift, gate))
        return prompt, sample

    xp, xs = x_prompt, x_sample.reshape(1, n_s, D)
    mod_p, mod_s = modulation(0)
    hp = _prenorm(xp, mod_p[0], mod_p[1], g_norm[0], tm)
    hs = _prenorm(xs, mod_s[0], mod_s[1], g_norm[0], n_s)
    stacked_p = None
    outs = {n: [] for n in ("sp", "ks", "vs", "iks", "ss")}
    for l in range(depth):
        final = l == depth - 1
        nxt_p, nxt_s = (None, None) if final else modulation(l + 1)
        g_next = g_final if final else g_norm[l + 1]
        ba = b_a2[l].reshape(1, -1)

        p1 = _inproj(hp, w1, l, F32, tm, 2048, "inproj_gates")
        p3 = _inproj(hp, w3, l, BF16, tm, 1024, "inproj_queries")
        stacked_p, ga32, kb, ikb, vb = _inproj_kv(hp, w2, l, stacked_p, depth, l, tm)
        s0 = jnp.zeros((B, H, dk, dv), F32)
        o_a, s_new = _gla(p1, ga32, gla_col, s0, wa[l], ba, g_gla[l], gla_consts, 128, None)
        o_b = _dsa_prompt(p1, p3, ga32, ikb, kb, vb, col1, col3, topk_p, 128)
        res = _outproj(o_a, o_b, p1, col1, xp, mod_p[2], wpa, wpb, wo, l, g_next,
                       None if final else nxt_p[:2], 256)
        xp, hp = (res, None) if final else res
        outs["sp"].append(s_new)

        q1 = _inproj(hs, w1, l, F32, n_s, 2048, "inproj_gates_s")
        q3 = _inproj(hs, w3, l, BF16, n_s, 1024, "inproj_queries_s")
        (k_s, v_s, ik_s), ga_s, _, _, _ = _inproj_kv(hs, w2, l, None, 1, 0, n_s)
        pad_t = lambda a: jnp.pad(a.reshape(Bs, Ts, -1), ((0, 0), (0, ts_pad - Ts), (0, 0)))
        o_a_s, s_new_s = _gla(pad_t(q1[0, :, :col1["az"]]), pad_t(ga_s[0]), gla_col, state_gla[l], wa[l], ba,
                              g_gla[l], gla_consts, ts_pad, Ts)
        o_a_s = o_a_s[:, :Ts].reshape(1, n_s, -1)

        iq = q3[0, :, col3["iq"]:col3["iq"] + IDX_HEADS * LANES].reshape(Bs, Ts, IDX_HEADS, LANES)
        iq_rows = jnp.transpose(iq, (0, 2, 1, 3)).reshape(Bs, IDX_HEADS * Ts, LANES)
        iw = ga_s[0, :, GLA_RANK:GLA_RANK + IDX_HEADS].reshape(Bs, Ts, IDX_HEADS)
        wb = jnp.broadcast_to(jnp.transpose(iw, (0, 2, 1)).reshape(Bs, IDX_HEADS * Ts, 1), (Bs, IDX_HEADS * Ts, LANES))
        ik_new = ik_s.reshape(Bs, Ts, LANES)
        k_new = k_s.reshape(Bs, Ts, aw)
        v_new = v_s.reshape(Bs, Ts, aw)
        aq = q3[0, :, col3["aq"]:col3["aq"] + aw].reshape(Bs, Ts, aw)
        az = q1[0, :, col1["az"]:col1["az"] + aw].reshape(Bs, Ts, aw)
        scores = _sample_scores(page_table, iq_rows, wb, cache_kidx, l, score_pages)
        o_b_s = _sample_attn(page_table, scores, iq_rows, wb, ik_new, aq, k_new, v_new, az, cache_k, cache_v,
                             l, topk_s, attn_pages)
        res = _outproj(o_a_s, o_b_s.reshape(1, n_s, aw), q1, col1, xs, mod_s[2], wpa, wpb, wo, l, g_next,
                       None if final else nxt_s[:2], n_s)
        xs, hs = (res, None) if final else res
        outs["ks"].append(k_new.reshape(Bs, Ts, ATT_HEADS, LANES))
        outs["vs"].append(v_new.reshape(Bs, Ts, ATT_HEADS, LANES))
        outs["iks"].append(ik_new)
        outs["ss"].append(s_new_s)
        mod_p, mod_s = nxt_p, nxt_s

    k_p, v_p, ik_p = stacked_p
    st = lambda n: jnp.stack(outs[n])
    return (xp, xs.reshape(Bs, Ts, D), k_p.reshape(depth, B, T, ATT_HEADS, LANES),
            v_p.reshape(depth, B, T, ATT_HEADS, LANES), ik_p, st("sp"), st("ks"), st("vs"), st("iks"), st("ss"))
```

```python
import functools

import numpy as np
import jax
import jax.numpy as jnp
from jax import lax
from jax.experimental import pallas as pl
from jax.experimental.pallas import tpu as pltpu

F32 = jnp.float32
BF16 = jnp.bfloat16
I32 = jnp.int32

EPS = 1e-6
PAGE_SIZE = 128
GLA_HEADS = 4
GLA_RANK = 16
GLA_TAU = 16.0
GLA_CHUNK = 64
ATT_HEADS = 8
IDX_HEADS = 16
TOPK_MAX = 256

LANES = 128
SUBLANES = 8
VMEM_LIMIT_BYTES = 56 * 1024 * 1024

NEG_BIG = -1e30
INT_MIN = -2147483648
KEY_NEG_INF = -2139095041


def _cparams(sem):
    return pltpu.CompilerParams(dimension_semantics=sem, vmem_limit_bytes=VMEM_LIMIT_BYTES)


def _dot(a, b):
    return jnp.dot(a, b, preferred_element_type=F32)


def _dot_nt(a, b):
    return lax.dot_general(a, b, (((1,), (1,)), ((), ())), preferred_element_type=F32)


def _sigmoid(x):
    return 1.0 / (1.0 + jnp.exp(-x))


def _sort_key(x):
    bits = pltpu.bitcast(x + 0.0, I32)
    return jnp.where(bits >= 0, bits, bits ^ jnp.int32(0x7FFFFFFF))


def _mod_kernel(c_ref, w_ref, b_ref, o_ref):
    o_ref[0] = _dot(c_ref[...].astype(BF16), w_ref[0].astype(BF16)) + b_ref[0]


def _modulation(c_all, w_c, b_c):
    depth, d, n = w_c.shape
    rows = c_all.shape[0]
    tn = 768
    return pl.pallas_call(
        _mod_kernel,
        grid=(depth, n // tn),
        in_specs=[pl.BlockSpec((rows, d), lambda l, j: (0, 0)),
                  pl.BlockSpec((1, d, tn), lambda l, j: (l, 0, j)),
                  pl.BlockSpec((1, 1, tn), lambda l, j: (l, 0, j))],
        out_specs=pl.BlockSpec((1, rows, tn), lambda l, j: (l, 0, j)),
        out_shape=jax.ShapeDtypeStruct((depth, rows, n), F32),
        compiler_params=_cparams(("parallel", "parallel")),
        name="adaln_modulation",
    )(c_all, w_c, b_c.reshape(depth, 1, n))


def _adaln_norm(x, g, scale, shift):
    y = x * lax.rsqrt(jnp.mean(x * x, axis=-1, keepdims=True) + EPS) * g
    return y * (1.0 + scale) + shift


def _row_spec(a, tm, ngrid):
    rb = 1 if a.shape[1] == 1 else tm
    if ngrid == 2:
        return pl.BlockSpec((1, rb, a.shape[2]), (lambda b, i: (b, 0, 0)) if rb == 1 else (lambda b, i: (b, i, 0)))
    return pl.BlockSpec((1, rb, a.shape[2]), (lambda b, i, j: (b, 0, 0)) if rb == 1 else (lambda b, i, j: (b, i, 0)))


def _prenorm_kernel(x_ref, sc_ref, sh_ref, g_ref, o_ref):
    o_ref[0] = _adaln_norm(x_ref[0], g_ref[...], sc_ref[0], sh_ref[0]).astype(o_ref.dtype)


def _prenorm(x, scale, shift, g, tm):
    B, T, D = x.shape
    return pl.pallas_call(
        _prenorm_kernel,
        grid=(B, T // tm),
        in_specs=[pl.BlockSpec((1, tm, D), lambda b, i: (b, i, 0)), _row_spec(scale, tm, 2), _row_spec(shift, tm, 2),
                  pl.BlockSpec((1, D), lambda b, i: (0, 0))],
        out_specs=pl.BlockSpec((1, tm, D), lambda b, i: (b, i, 0)),
        out_shape=jax.ShapeDtypeStruct((B, T, D), BF16),
        compiler_params=_cparams(("parallel", "parallel")),
        name="prenorm",
    )(x, scale, shift, g.reshape(1, D))


def _inproj_kernel(h_ref, w_ref, o_ref):
    o_ref[0] = _dot(h_ref[0], w_ref[...]).astype(o_ref.dtype)


def _inproj(h, w_all, layer, out_dtype, tm, tn, name):
    B, T, D = h.shape
    N = w_all.shape[2]
    return pl.pallas_call(
        _inproj_kernel,
        grid=(B, T // tm, N // tn),
        in_specs=[pl.BlockSpec((1, tm, D), lambda b, i, j: (b, i, 0)),
                  pl.BlockSpec((None, D, tn), lambda b, i, j: (layer, 0, j))],
        out_specs=pl.BlockSpec((1, tm, tn), lambda b, i, j: (b, i, j)),
        out_shape=jax.ShapeDtypeStruct((B, T, N), out_dtype),
        compiler_params=_cparams(("parallel", "parallel", "parallel")),
        name=name,
    )(h, w_all)


def _inproj_kv_kernel(h_ref, w_ref, *refs, aw, n_alias):
    k32, ik32, v32, ga32, kb, ikb, vb = refs[n_alias:]
    r = _dot(h_ref[0], w_ref[...])
    j = pl.program_id(2)

    @pl.when(j == 0)
    def _():
        k32[0] = r[:, :aw]
        ik32[0] = r[:, aw:]
        kb[0] = r[:, :aw].astype(BF16)
        ikb[0] = r[:, aw:].astype(BF16)

    @pl.when(j == 1)
    def _():
        v32[0] = r[:, :aw]
        ga32[0] = r[:, aw:]
        vb[0] = r[:, :aw].astype(BF16)


def _inproj_kv(h, w_all, layer, stacked, depth_out, layer_out, tm):
    B, T, D = h.shape
    aw = ATT_HEADS * LANES
    tn = aw + LANES
    stacked = () if stacked is None else tuple(stacked)
    any_spec = pl.BlockSpec(memory_space=pl.ANY)
    lay = lambda w: pl.BlockSpec((None, 1, tm, w), lambda b, i, j: (layer_out, b, i, 0))
    per = lambda w: pl.BlockSpec((1, tm, w), lambda b, i, j: (b, i, 0))
    f32s = lambda w: jax.ShapeDtypeStruct((depth_out, B, T, w), F32)
    outs = pl.pallas_call(
        functools.partial(_inproj_kv_kernel, aw=aw, n_alias=len(stacked)),
        grid=(B, T // tm, 2),
        in_specs=[pl.BlockSpec((1, tm, D), lambda b, i, j: (b, i, 0)),
                  pl.BlockSpec((None, D, tn), lambda b, i, j: (layer, 0, j))] + [any_spec] * len(stacked),
        out_specs=[lay(aw), lay(LANES), lay(aw), per(LANES), per(aw), per(LANES), per(aw)],
        out_shape=[f32s(aw), f32s(LANES), f32s(aw), jax.ShapeDtypeStruct((B, T, LANES), F32),
                   jax.ShapeDtypeStruct((B, T, aw), BF16), jax.ShapeDtypeStruct((B, T, LANES), BF16),
                   jax.ShapeDtypeStruct((B, T, aw), BF16)],
        input_output_aliases={2: 0, 3: 2, 4: 1} if stacked else {},
        compiler_params=_cparams(("parallel", "parallel", "arbitrary")),
        name="inproj_kv",
    )(h, w_all, *stacked)
    k32, ik32, v32, ga32, kb, ikb, vb = outs
    return (k32, v32, ik32), ga32, kb, ikb, vb


def _gla_constants(C):
    r = np.arange(C)[:, None]
    j = np.arange(C)[None, :]
    blocks = [(j <= r), (j > r)]
    masks = []
    w = C // 2
    while w >= 1:
        mid = (r // (2 * w)) * (2 * w) + w
        upper = r >= mid
        blocks.append(np.where(upper, (j >= mid) & (j <= r), (j > r) & (j < mid)))
        t, s = r, j
        same = (t // (2 * w)) == (s // (2 * w))
        t_up = t >= (t // (2 * w)) * (2 * w) + w
        s_lo = s < (s // (2 * w)) * (2 * w) + w
        masks.append(same & t_up & s_lo)
        w //= 2
    masks.append(r == j)
    W = np.concatenate(blocks, axis=0).astype(np.float32)
    M = np.stack(masks).astype(np.float32)
    return jnp.asarray(W, dtype=BF16), jnp.asarray(M, dtype=F32)


def _gla_kernel(q_ref, k_ref, v_ref, z_ref, ga_ref, wa_ref, ba_ref, gg_ref, s0_ref, W_ref, M_ref,
                o_ref, sout_ref, S_ref, *, C, nchunk, nlev, valid, H, dk, dv):
    ci = pl.program_id(1)

    @pl.when(ci == 0)
    def _():
        S_ref[...] = s0_ref[0]

    W = W_ref[...]
    for c in range(nchunk):
        rows = pl.ds(c * C, C)
        zz = _dot(ga_ref[0, rows, :].astype(BF16), wa_ref[...]) + ba_ref[...]
        g = -(jnp.maximum(-zz, 0.0) + jnp.log1p(jnp.exp(-jnp.abs(zz)))) * (1.0 / GLA_TAU)
        if valid is not None:
            tok = ci * (nchunk * C) + c * C + lax.broadcasted_iota(I32, g.shape, 0)
            g = jnp.where(tok < valid, g, 0.0)
        g1 = g.astype(BF16)
        r1 = g - g1.astype(F32)
        g2 = r1.astype(BF16)
        g3 = (r1 - g2.astype(F32)).astype(BF16)
        E_all = jnp.exp(_dot(W, g1) + _dot(W, g2) + _dot(W, g3))
        for h in range(H):
            kc = slice(h * dk, (h + 1) * dk)
            vc = slice(h * dv, (h + 1) * dv)
            E = E_all[:, kc]
            qs = q_ref[0, rows, kc] * (dk ** -0.5)
            k = k_ref[0, rows, kc]
            v = v_ref[0, rows, vc].astype(BF16)
            e_b = E[0:C]
            e_rest = E[C:2 * C]
            A = jnp.where(M_ref[nlev] > 0, _dot_nt(qs.astype(BF16), k.astype(BF16)), 0.0)
            for l in range(nlev):
                e_l = E[(2 + l) * C:(3 + l) * C]
                A = A + jnp.where(M_ref[l] > 0, _dot_nt((qs * e_l).astype(BF16), (k * e_l).astype(BF16)), 0.0)
            S = S_ref[h]
            o = _dot(A.astype(BF16), v) + _dot((qs * e_b).astype(BF16), S.astype(BF16))
            k_dec = k * e_rest
            e_last = e_b.T[:, C - 1:C]
            S_ref[h] = e_last * S + _dot(k_dec.T.astype(BF16), v)
            y = o * lax.rsqrt(jnp.mean(o * o, axis=-1, keepdims=True) + EPS) * gg_ref[h:h + 1, :]
            zg = z_ref[0, rows, vc]
            o_ref[0, rows, vc] = (y * (zg * _sigmoid(zg))).astype(o_ref.dtype)

    @pl.when(ci == pl.num_programs(1) - 1)
    def _():
        sout_ref[0] = S_ref[...]


def _gla(p1, p2, col, s0, wa, ba, gg, consts, tc, valid):
    B, T, _ = p1.shape
    H, dk, dv = s0.shape[1], s0.shape[2], s0.shape[3]
    W, M = consts
    C = GLA_CHUNK
    nlev = M.shape[0] - 1
    qo, ko, vo, zo = (col[n] // (H * w) for n, w in (("gq", dk), ("gk", dk), ("gv", dv), ("gz", dv)))
    gao = col["gaiw"] // LANES
    kern = functools.partial(_gla_kernel, C=C, nchunk=tc // C, nlev=nlev, valid=valid, H=H, dk=dk, dv=dv)
    return pl.pallas_call(
        kern,
        grid=(B, T // tc),
        in_specs=[pl.BlockSpec((1, tc, H * dk), lambda b, i: (b, i, qo)),
                  pl.BlockSpec((1, tc, H * dk), lambda b, i: (b, i, ko)),
                  pl.BlockSpec((1, tc, H * dv), lambda b, i: (b, i, vo)),
                  pl.BlockSpec((1, tc, H * dv), lambda b, i: (b, i, zo)),
                  pl.BlockSpec((1, tc, LANES), lambda b, i: (b, i, gao)),
                  pl.BlockSpec((LANES, H * dk), lambda b, i: (0, 0)),
                  pl.BlockSpec((1, H * dk), lambda b, i: (0, 0)),
                  pl.BlockSpec((H, dv), lambda b, i: (0, 0)),
                  pl.BlockSpec((1, H, dk, dv), lambda b, i: (b, 0, 0, 0)),
                  pl.BlockSpec(W.shape, lambda b, i: (0, 0)),
                  pl.BlockSpec(M.shape, lambda b, i: (0, 0, 0))],
        out_specs=[pl.BlockSpec((1, tc, H * dv), lambda b, i: (b, i, 0)),
                   pl.BlockSpec((1, H, dk, dv), lambda b, i: (b, 0, 0, 0))],
        out_shape=[jax.ShapeDtypeStruct((B, T, H * dv), BF16),
                   jax.ShapeDtypeStruct((B, H, dk, dv), F32)],
        scratch_shapes=[pltpu.VMEM((H, dk, dv), F32)],
        compiler_params=_cparams(("parallel", "arbitrary")),
        name="gla_branch",
    )(p1, p1, p1, p1, p2, wa, ba, gg, s0, W, M)


def _kth_key(count_ge, kk, rows):
    def body(i, u):
        cand = u | (jnp.int32(1) << (31 - i))
        ok = count_ge(cand ^ jnp.int32(INT_MIN)) >= kk
        return jnp.where(ok, cand, u)
    u = lax.fori_loop(0, 32, body, jnp.zeros((rows, 1), I32))
    return u ^ jnp.int32(INT_MIN)


def _kth_key_groups(keys, kk):
    def body(i, us):
        bit = jnp.int32(1) << (31 - i)
        cands = [u | bit for u in us]
        oks = [_count(key >= (c ^ jnp.int32(INT_MIN))) >= kk for key, c in zip(keys, cands)]
        return tuple(jnp.where(ok, c, u) for ok, c, u in zip(oks, cands, us))
    us = lax.fori_loop(0, 32, body, tuple(jnp.zeros((k.shape[0], 1), I32) for k in keys), unroll=2)
    return jnp.concatenate([u ^ jnp.int32(INT_MIN) for u in us], axis=0)


def _tie_cutoff(count_eq_below, need, rows, nbits):
    def body(i, c):
        cand = c | (jnp.int32(1) << (nbits - 1 - i))
        ok = count_eq_below(cand) <= need
        return jnp.where(ok, cand, c)
    return lax.fori_loop(0, nbits, body, jnp.zeros((rows, 1), I32))


def _count(mask):
    return jnp.sum(jnp.where(mask, 1.0, 0.0), axis=1, keepdims=True)


def _dsa_block(slope_ref, iq_ref, ik_ref, iw_ref, q_ref, k_ref, v_ref, z_ref, o_ref, bias_ref, *, S, tq, topk, iw_off,
               scale, idx_scale):
    q0 = pl.program_id(1) * tq
    pos = q0 + lax.broadcasted_iota(I32, (tq, S), 0)
    kpos = lax.broadcasted_iota(I32, (tq, S), 1)
    head_cols = lambda h: pl.ds(pl.multiple_of(h * LANES, LANES), LANES)

    if S <= topk:
        bias_ref[:, 0:S] = jnp.where(kpos <= pos, 0.0, NEG_BIG)
    else:
        iw = iw_ref[0]
        ik = ik_ref[0, 0:S, :]
        score = jnp.zeros((tq, S), F32)
        for h in range(IDX_HEADS):
            s = _dot_nt(iq_ref[0, :, h * LANES:(h + 1) * LANES], ik)
            score = score + jnp.maximum(s, 0.0) * (iw[:, iw_off + h:iw_off + h + 1] * idx_scale)
        key = _sort_key(jnp.where(kpos <= pos, score, -jnp.inf))

        kk = float(topk)
        ng = 2
        thr = _kth_key_groups([key[i * tq // ng:(i + 1) * tq // ng] for i in range(ng)], kk)
        thr_sel = jnp.maximum(thr, KEY_NEG_INF + 1)
        bias_ref[:, 0:S] = jnp.where(key >= thr_sel, 0.0, NEG_BIG)
        tied = jnp.where(_count(key >= thr_sel) > kk, 1.0, 0.0)

        @pl.when(jnp.max(tied) > 0.0)
        def _():
            need = kk - _count(key > thr)
            eq_pos = jnp.where(key == thr, kpos, jnp.int32(2 ** 30))
            cut = _tie_cutoff(lambda c_: _count(eq_pos < c_), need, tq, int(S).bit_length())
            keep_eq = jnp.where(key == thr, jnp.where(kpos < cut, 0.0, NEG_BIG), NEG_BIG)
            bias_ref[:, 0:S] = jnp.where(key >= thr_sel, jnp.where(key > thr, 0.0, keep_eq), NEG_BIG)

    log2e = 1.4426950408889634
    krel = (lax.broadcasted_iota(I32, (1, S), 1) - q0).astype(F32)

    def attend(h, carry):
        cols = head_cols(h)
        qk = _dot_nt(q_ref[0, :, cols], k_ref[0, 0:S, cols])
        logits = qk * (scale * log2e) + krel * (slope_ref[h] * log2e) + bias_ref[:, 0:S]
        m = jnp.max(logits, axis=1, keepdims=True)
        p = jnp.exp2(logits - m)
        l = jnp.sum(p, axis=1, keepdims=True)
        out = _dot(p.astype(BF16), v_ref[0, 0:S, cols]) / l
        zg = z_ref[0, :, cols]
        o_ref[0, :, cols] = (out * (zg * _sigmoid(zg))).astype(o_ref.dtype)
        return carry

    lax.fori_loop(0, ATT_HEADS, attend, 0, unroll=4)


def _key_extents(T, tq, topk):
    marks = sorted({min(T, max(tq, topk)), T // 4, 3 * T // 8, T // 2, 3 * T // 4, T})
    return [m for m in marks if m % tq == 0 and m >= tq]


def _dsa_kernel(*refs, tq, T, extents, **kw):
    last = (pl.program_id(1) + 1) * tq
    lo = 0
    for S in extents:
        pl.when((last > lo) & (last <= S))(functools.partial(_dsa_block, *refs, S=S, tq=tq, **kw))
        lo = S


def _dsa_prompt(p1, p3, ga32, ikb, kb, vb, col1, col3, topk, tq):
    B, T, _ = p1.shape
    aw = ATT_HEADS * LANES
    kern = functools.partial(_dsa_kernel, tq=tq, T=T, extents=_key_extents(T, tq, topk), topk=topk, iw_off=GLA_RANK,
                             scale=LANES ** -0.5, idx_scale=(LANES * IDX_HEADS) ** -0.5)
    iq_o = col3["iq"] // (IDX_HEADS * LANES)
    slopes = jnp.asarray(2.0 ** -(np.arange(ATT_HEADS) + 1.0), F32)
    return pl.pallas_call(
        kern,
        grid=(B, T // tq),
        in_specs=[pl.BlockSpec(memory_space=pltpu.SMEM),
                  pl.BlockSpec((1, tq, IDX_HEADS * LANES), lambda b, i: (b, i, iq_o)),
                  pl.BlockSpec((1, T, LANES), lambda b, i: (b, 0, 0)),
                  pl.BlockSpec((1, tq, LANES), lambda b, i: (b, i, 0)),
                  pl.BlockSpec((1, tq, aw), lambda b, i: (b, i, col3["aq"] // aw)),
                  pl.BlockSpec((1, T, aw), lambda b, i: (b, 0, 0)),
                  pl.BlockSpec((1, T, aw), lambda b, i: (b, 0, 0)),
                  pl.BlockSpec((1, tq, aw), lambda b, i: (b, i, col1["az"] // aw))],
        out_specs=pl.BlockSpec((1, tq, aw), lambda b, i: (b, i, 0)),
        out_shape=jax.ShapeDtypeStruct((B, T, aw), BF16),
        scratch_shapes=[pltpu.VMEM((tq, T), F32)],
        compiler_params=_cparams(("parallel", "arbitrary")),
        name="dsa_prompt",
    )(slopes, p3, ikb, ga32, p3, kb, vb, p1)


def _page_specs(n, block, layer, pages_per_step):
    def spec(g):
        return pl.BlockSpec(block, lambda b, p, pt: (layer, pt[b, p * pages_per_step + g]) + (0,) * (len(block) - 2))
    return [spec(g) for g in range(n)]


def _sample_scores_kernel(pt_ref, iq_ref, wb_ref, *rest, idx_scale):
    kidx_refs, o_ref = rest[:-1], rest[-1]
    iq = iq_ref[0]
    w = wb_ref[0] * idx_scale
    tq = o_ref.shape[1]
    for g, kidx_ref in enumerate(kidx_refs):
        s = jnp.maximum(_dot_nt(iq, kidx_ref[...].astype(BF16)), 0.0) * w
        o_ref[0, :, g * PAGE_SIZE:(g + 1) * PAGE_SIZE] = jnp.sum(s.reshape(IDX_HEADS, tq, PAGE_SIZE), axis=0)


def _sample_scores(page_table, iq_rows, wb, cache_kidx, layer, pages_per_step):
    B, n_pages = page_table.shape
    tq = iq_rows.shape[1] // IDX_HEADS
    G = pages_per_step
    kern = functools.partial(_sample_scores_kernel, idx_scale=(LANES * IDX_HEADS) ** -0.5)
    return pl.pallas_call(
        kern,
        grid_spec=pltpu.PrefetchScalarGridSpec(
            num_scalar_prefetch=1,
            grid=(B, n_pages // G),
            in_specs=[pl.BlockSpec((1, IDX_HEADS * tq, LANES), lambda b, p, pt: (b, 0, 0)),
                      pl.BlockSpec((1, IDX_HEADS * tq, LANES), lambda b, p, pt: (b, 0, 0))]
                     + _page_specs(G, (None, None, PAGE_SIZE, LANES), layer, G),
            out_specs=pl.BlockSpec((1, tq, G * PAGE_SIZE), lambda b, p, pt: (b, 0, p)),
        ),
        out_shape=jax.ShapeDtypeStruct((B, tq, n_pages * PAGE_SIZE), F32),
        compiler_params=_cparams(("parallel", "arbitrary")),
        name="sample_scores",
    )(page_table, iq_rows, wb, *([cache_kidx] * G))


def _sample_attn_kernel(pt_ref, sc_ref, iq_ref, wb_ref, ikn_ref, q_ref, kn_ref, vn_ref, z_ref, *rest,
                        G, tq, past, topk, scale, idx_scale):
    ck_refs, cv_refs, o_ref = rest[:G], rest[G:2 * G], rest[2 * G]
    qs_ref, thr_ref, cut_ref, bnew_ref, m_ref, l_ref, acc_ref = rest[2 * G + 1:]
    p = pl.program_id(1)
    H = ATT_HEADS
    lane = lax.broadcasted_iota(I32, (tq, LANES), 1)
    row = lax.broadcasted_iota(I32, (tq, LANES), 0)

    @pl.when(p == 0)
    def _():
        q = q_ref[0].astype(F32)
        qs_ref[...] = jnp.concatenate([q[:, h * LANES:(h + 1) * LANES] for h in range(H)], axis=0)
        ikn = jnp.concatenate([ikn_ref[0], jnp.zeros((LANES - tq, LANES), F32)], axis=0).astype(BF16)
        s = jnp.maximum(_dot_nt(iq_ref[0], ikn), 0.0) * (wb_ref[0] * idx_scale)
        s_new = jnp.sum(s.reshape(IDX_HEADS, tq, LANES), axis=0)
        key_new = _sort_key(jnp.where((lane <= row) & (lane < tq), s_new, -jnp.inf))
        key_past = _sort_key(sc_ref[0])

        kk = float(topk)
        count_ge = lambda t: _count(key_past >= t) + _count(key_new >= t)
        thr = _kth_key(count_ge, kk, tq)
        thr_ref[...] = jnp.broadcast_to(thr, thr_ref.shape)
        cut_ref[...] = jnp.full(cut_ref.shape, past + LANES, I32)
        tied = jnp.where((count_ge(thr) > kk) & (thr > KEY_NEG_INF), 1.0, 0.0)

        @pl.when(jnp.max(tied) > 0.0)
        def _():
            need = kk - _count(key_past > thr) - _count(key_new > thr)
            idx_past = lax.broadcasted_iota(I32, key_past.shape, 1)
            eq_past = key_past == thr
            eq_new = key_new == thr
            cnt = lambda c_: _count(eq_past & (idx_past < c_)) + _count(eq_new & (lane + past < c_))
            c = _tie_cutoff(cnt, need, tq, int(past + LANES).bit_length())
            cut_ref[...] = jnp.broadcast_to(c, cut_ref.shape)

        sel_new = (key_new > thr) | ((key_new == thr) & (lane + past < cut_ref[...]))
        bnew_ref[...] = jnp.where(sel_new & (lane <= row) & (lane < tq), 0.0, NEG_BIG)
        m_ref[...] = jnp.full(m_ref.shape, -jnp.inf, F32)
        l_ref[...] = jnp.zeros(l_ref.shape, F32)
        acc_ref[...] = jnp.zeros(acc_ref.shape, F32)

    thr = thr_ref[...]
    cut = cut_ref[...]
    bias, dist = [], []
    for g in range(G):
        first = pl.multiple_of((p * G + g) * PAGE_SIZE, PAGE_SIZE)
        key_p = _sort_key(sc_ref[0, :, pl.ds(first, PAGE_SIZE)])
        kpos = lane + first
        keep_eq = jnp.where(key_p == thr, jnp.where(kpos < cut, 0.0, NEG_BIG), NEG_BIG)
        bias.append(jnp.where(key_p > thr, 0.0, keep_eq))
        dist.append((row + past - kpos).astype(F32))
    head_rows = [pl.ds(h, PAGE_SIZE, stride=H) for h in range(H)]
    qs = qs_ref[...]
    logits = []
    for h in range(H):
        q_h = qs[h * tq:(h + 1) * tq].astype(BF16)
        logits.append([_dot_nt(q_h, ck_refs[g][head_rows[h], :].astype(BF16)) * scale
                       - (2.0 ** -(h + 1)) * dist[g] + bias[g] for g in range(G)])
    m_old = m_ref[:, 0:1]
    page_max = jnp.concatenate([functools.reduce(jnp.maximum, lg) for lg in logits], axis=0)
    m_new = jnp.maximum(m_old, jnp.max(page_max, axis=1, keepdims=True))
    alpha = jnp.exp(m_old - m_new)
    pe = [[jnp.exp(logits[h][g] - m_new[h * tq:(h + 1) * tq]) for g in range(G)] for h in range(H)]
    psum = jnp.concatenate([functools.reduce(jnp.add, ph) for ph in pe], axis=0)
    pv = [functools.reduce(jnp.add, [_dot(pe[h][g].astype(BF16), cv_refs[g][head_rows[h], :].astype(BF16))
                                     for g in range(G)]) for h in range(H)]
    l_new = alpha * l_ref[:, 0:1] + jnp.sum(psum, axis=1, keepdims=True)
    acc_ref[...] = alpha * acc_ref[...] + jnp.concatenate(pv, axis=0)
    l_ref[...] = jnp.broadcast_to(l_new, l_ref.shape)
    m_ref[...] = jnp.broadcast_to(m_new, m_ref.shape)

    @pl.when(p == pl.num_programs(1) - 1)
    def _():
        zpad = jnp.zeros((LANES - tq, LANES), F32)
        bnew = bnew_ref[...]
        dist = (row - lane).astype(F32)
        qs = qs_ref[...].astype(BF16)
        lg = []
        for h in range(H):
            cols = slice(h * LANES, (h + 1) * LANES)
            kn = jnp.concatenate([kn_ref[0, :, cols], zpad], axis=0).astype(BF16)
            qk = _dot_nt(qs, kn)[h * tq:(h + 1) * tq]
            lg.append(qk * scale - (2.0 ** -(h + 1)) * dist + bnew)
        logits_new = jnp.concatenate(lg, axis=0)
        m_old = m_ref[:, 0:1]
        m_new = jnp.maximum(m_old, jnp.max(logits_new, axis=1, keepdims=True))
        alpha = jnp.exp(m_old - m_new)
        pe = jnp.exp(logits_new - m_new)
        l_fin = alpha * l_ref[:, 0:1] + jnp.sum(pe, axis=1, keepdims=True)
        acc = alpha * acc_ref[...]
        pe = pe.astype(BF16)
        for h in range(H):
            cols = slice(h * LANES, (h + 1) * LANES)
            vn = jnp.concatenate([vn_ref[0, :, cols], zpad], axis=0).astype(BF16)
            rows = slice(h * tq, (h + 1) * tq)
            out = (acc[rows] + _dot(pe, vn)[rows]) / l_fin[rows]
            zg = z_ref[0, :, cols]
            o_ref[0, :, cols] = (out * (zg * _sigmoid(zg))).astype(o_ref.dtype)


def _sample_attn(page_table, scores, iq_rows, wb, ik_new, q, k_new, v_new, z, cache_k, cache_v, layer, topk,
                 pages_per_step):
    B, n_pages = page_table.shape
    tq = q.shape[1]
    H = ATT_HEADS
    aw = H * LANES
    past = n_pages * PAGE_SIZE
    G = pages_per_step
    kern = functools.partial(_sample_attn_kernel, G=G, tq=tq, past=past, topk=topk, scale=LANES ** -0.5,
                             idx_scale=(LANES * IDX_HEADS) ** -0.5)
    per_b = lambda b, p, pt: (b, 0, 0)
    page_block = (None, None, PAGE_SIZE * H, LANES)
    cache_k, cache_v = (c.reshape(c.shape[0], c.shape[1], PAGE_SIZE * H, LANES) for c in (cache_k, cache_v))
    return pl.pallas_call(
        kern,
        grid_spec=pltpu.PrefetchScalarGridSpec(
            num_scalar_prefetch=1,
            grid=(B, n_pages // G),
            in_specs=[pl.BlockSpec((1, tq, past), per_b),
                      pl.BlockSpec((1, IDX_HEADS * tq, LANES), per_b),
                      pl.BlockSpec((1, IDX_HEADS * tq, LANES), per_b),
                      pl.BlockSpec((1, tq, LANES), per_b),
                      pl.BlockSpec((1, tq, aw), per_b),
                      pl.BlockSpec((1, tq, aw), per_b),
                      pl.BlockSpec((1, tq, aw), per_b),
                      pl.BlockSpec((1, tq, aw), per_b)]
                     + _page_specs(G, page_block, layer, G) + _page_specs(G, page_block, layer, G),
            out_specs=pl.BlockSpec((1, tq, aw), per_b),
            scratch_shapes=[pltpu.VMEM((H * tq, LANES), F32),
                            pltpu.VMEM((tq, LANES), I32),
                            pltpu.VMEM((tq, LANES), I32),
                            pltpu.VMEM((tq, LANES), F32),
                            pltpu.VMEM((H * tq, LANES), F32),
                            pltpu.VMEM((H * tq, LANES), F32),
                            pltpu.VMEM((H * tq, LANES), F32)],
        ),
        out_shape=jax.ShapeDtypeStruct((B, tq, aw), BF16),
        compiler_params=_cparams(("parallel", "arbitrary")),
        name="sample_attention",
    )(page_table, scores, iq_rows, wb, ik_new, q, k_new, v_new, z, *([cache_k] * G), *([cache_v] * G))


def _outproj_kernel(oa_ref, ob_ref, ma_ref, mb_ref, x_ref, gate_ref, wpa_ref, wpb_ref, wo_ref, g_ref, *rest, final):
    pa = _dot(oa_ref[0], wpa_ref[...])
    pb = _dot(ob_ref[0], wpb_ref[...])
    merged = _sigmoid(ma_ref[0]) * pa + _sigmoid(mb_ref[0]) * pb
    x = x_ref[0] + gate_ref[0] * _dot(merged.astype(BF16), wo_ref[...])
    if final:
        (o_ref,) = rest
        o_ref[0] = x * lax.rsqrt(jnp.mean(x * x, axis=-1, keepdims=True) + EPS) * g_ref[...]
    else:
        sc_ref, sh_ref, o_ref, h_ref = rest
        o_ref[0] = x
        h_ref[0] = _adaln_norm(x, g_ref[...], sc_ref[0], sh_ref[0]).astype(h_ref.dtype)


def _outproj(o_a, o_b, p1, col1, x, gate, w_pa, w_pb, w_out, layer, g, nxt, tm):
    B, T, D = x.shape
    final = nxt is None
    const = lambda b, i: (0, 0)
    wspec = lambda w: pl.BlockSpec((None,) + w.shape[1:], lambda b, i: (layer, 0, 0), pipeline_mode=pl.Buffered(1))
    xspec = pl.BlockSpec((1, tm, D), lambda b, i: (b, i, 0))
    extra = [] if final else [_row_spec(nxt[0], tm, 2), _row_spec(nxt[1], tm, 2)]
    return pl.pallas_call(
        functools.partial(_outproj_kernel, final=final),
        grid=(B, T // tm),
        in_specs=[pl.BlockSpec((1, tm, o_a.shape[2]), lambda b, i: (b, i, 0)),
                  pl.BlockSpec((1, tm, o_b.shape[2]), lambda b, i: (b, i, 0)),
                  pl.BlockSpec((1, tm, D), lambda b, i: (b, i, col1["m_a"] // D)),
                  pl.BlockSpec((1, tm, D), lambda b, i: (b, i, col1["m_b"] // D)),
                  xspec, _row_spec(gate, tm, 2), wspec(w_pa), wspec(w_pb), wspec(w_out),
                  pl.BlockSpec((1, D), const)] + extra,
        out_specs=xspec if final else [xspec, xspec],
        out_shape=(jax.ShapeDtypeStruct((B, T, D), F32) if final else
                   [jax.ShapeDtypeStruct((B, T, D), F32), jax.ShapeDtypeStruct((B, T, D), BF16)]),
        compiler_params=_cparams(("parallel", "parallel")),
        name="merge_outproj",
    )(o_a, o_b, p1, p1, x, gate, w_pa, w_pb, w_out, g.reshape(1, D), *([] if final else nxt))


def _column_groups(d):
    gqk, gv, att, iqw = d // 4, d // 2, d // 2, d
    sizes = [("gq", gqk), ("gk", gqk), ("gv", gv), ("gz", gv), ("ga", GLA_RANK), ("aq", att), ("ak", att),
             ("av", att), ("az", att), ("iq", iqw), ("ik", LANES), ("iw", IDX_HEADS), ("m_a", d), ("m_b", d)]
    src, o = {}, 0
    for n, s in sizes:
        src[n] = (o, o + s)
        o += s
    groups = (("gq", "gk", "gv", "gz", "az", "m_a", "m_b"), ("ak", "ik", "av", "gaiw"), ("iq", "aq"))
    return src, groups


def _group_layout(src, names):
    offs, o = {}, 0
    for n in names:
        offs[n] = o
        o += LANES if n == "gaiw" else src[n][1] - src[n][0]
    return offs, o


def _regroup_kernel(wt_ref, *out_refs, src, groups):
    rows = 4 * LANES
    for names, o_ref in zip(groups, out_refs):
        offs, _ = _group_layout(src, names)
        for n in names:
            if n == "gaiw":
                pad = jnp.zeros((LANES - GLA_RANK - IDX_HEADS, LANES), F32)
                blk = jnp.concatenate([wt_ref[src["ga"][0]:src["ga"][1], :], wt_ref[src["iw"][0]:src["iw"][1], :],
                                       pad], axis=0)
                o_ref[:, offs[n]:offs[n] + LANES] = blk.T.astype(BF16)
            else:
                a, b = src[n]
                for r in range(a, b, rows):
                    e = min(b, r + rows)
                    o_ref[:, offs[n] + r - a:offs[n] + e - a] = wt_ref[r:e, :].T.astype(BF16)


def _regroup_weights(w, src, groups):
    depth, D, n_in = w.shape
    widths = [_group_layout(src, names)[1] for names in groups]
    return pl.pallas_call(
        functools.partial(_regroup_kernel, src=src, groups=groups),
        grid=(depth, D // LANES),
        in_specs=[pl.BlockSpec((None, n_in, LANES), lambda l, i: (l, 0, i))],
        out_specs=[pl.BlockSpec((None, LANES, n), lambda l, i: (l, i, 0)) for n in widths],
        out_shape=[jax.ShapeDtypeStruct((depth, D, n), BF16) for n in widths],
        compiler_params=_cparams(("parallel", "parallel")),
        name="regroup_weights",
    )(jnp.swapaxes(w, 1, 2))


def kernel(x_prompt, x_sample, cache_k, cache_v, cache_kidx, state_gla, page_table, c_prompt, c_sample, w_c, b_c,
           g_norm, w_in, w_a2, b_a2, g_gla, w_pa, w_pb, w_out, g_final):
    depth = w_in.shape[0]
    B, T, D = x_prompt.shape
    Bs, Ts, _ = x_sample.shape
    H, dk, dv = state_gla.shape[2], state_gla.shape[3], state_gla.shape[4]
    n_pages = page_table.shape[1]
    past = n_pages * PAGE_SIZE
    aw = ATT_HEADS * LANES
    n_s = Bs * Ts
    src, groups = _column_groups(D)

    rows = -(-(B + Bs) // SUBLANES) * SUBLANES
    c_all = jnp.concatenate([c_prompt, c_sample, jnp.zeros((rows - B - Bs, D), F32)], axis=0)
    mod = _modulation(c_all, w_c, b_c)

    w1, w2, w3 = _regroup_weights(w_in, src, groups)
    col1, col3 = _group_layout(src, groups[0])[0], _group_layout(src, groups[2])[0]
    wa = jnp.concatenate([w_a2, jnp.zeros((depth, LANES - GLA_RANK, w_a2.shape[2]), F32)], axis=1).astype(BF16)
    wpa, wpb, wo = w_pa.astype(BF16), w_pb.astype(BF16), w_out.astype(BF16)
    gla_col = {**col1, "gaiw": 0}

    gla_consts = _gla_constants(GLA_CHUNK)
    score_pages = min(16, n_pages)
    attn_pages = min(16, n_pages)
    topk_p = min(TOPK_MAX, T // 4)
    topk_s = min(TOPK_MAX, (past + Ts) // 4)
    ts_pad = GLA_CHUNK
    tm = min(T, 512)

    def modulation(l):
        shift, scale, gate = (mod[l, :, i * D:(i + 1) * D] for i in range(3))
        prompt = tuple(a[:B].reshape(B, 1, D) for a in (scale, shift, gate))
        sample = tuple(jnp.repeat(a[B:B + Bs], Ts, axis=0).reshape(1, n_s, D) for a in (scale, shift, gate))
        return prompt, sample

    xp, xs = x_prompt, x_sample.reshape(1, n_s, D)
    mod_p, mod_s = modulation(0)
    hp = _prenorm(xp, mod_p[0], mod_p[1], g_norm[0], tm)
    hs = _prenorm(xs, mod_s[0], mod_s[1], g_norm[0], n_s)
    stacked_p = None
    outs = {n: [] for n in ("sp", "ks", "vs", "iks", "ss")}
    for l in range(depth):
        final = l == depth - 1
        nxt_p, nxt_s = (None, None) if final else modulation(l + 1)
        g_next = g_final if final else g_norm[l + 1]
        ba = b_a2[l].reshape(1, -1)

        p1 = _inproj(hp, w1, l, F32, tm, 2048, "inproj_gates")
        p3 = _inproj(hp, w3, l, BF16, tm, 1024, "inproj_queries")
        stacked_p, ga32, kb, ikb, vb = _inproj_kv(hp, w2, l, stacked_p, depth, l, tm)
        s0 = jnp.zeros((B, H, dk, dv), F32)
        o_a, s_new = _gla(p1, ga32, gla_col, s0, wa[l], ba, g_gla[l], gla_consts, 128, None)
        o_b = _dsa_prompt(p1, p3, ga32, ikb, kb, vb, col1, col3, topk_p, 128)
        res = _outproj(o_a, o_b, p1, col1, xp, mod_p[2], wpa, wpb, wo, l, g_next,
                       None if final else nxt_p[:2], 256)
        xp, hp = (res, None) if final else res
        outs["sp"].append(s_new)

        q1 = _inproj(hs, w1, l, F32, n_s, 2048, "inproj_gates_s")
        q3 = _inproj(hs, w3, l, BF16, n_s, 1024, "inproj_queries_s")
        (k_s, v_s, ik_s), ga_s, _, _, _ = _inproj_kv(hs, w2, l, None, 1, 0, n_s)
        pad_t = lambda a: jnp.pad(a.reshape(Bs, Ts, -1), ((0, 0), (0, ts_pad - Ts), (0, 0)))
        o_a_s, s_new_s = _gla(pad_t(q1[0, :, :col1["az"]]), pad_t(ga_s[0]), gla_col, state_gla[l], wa[l], ba,
                              g_gla[l], gla_consts, ts_pad, Ts)
        o_a_s = o_a_s[:, :Ts].reshape(1, n_s, -1)

        iq = q3[0, :, col3["iq"]:col3["iq"] + IDX_HEADS * LANES].reshape(Bs, Ts, IDX_HEADS, LANES)
        iq_rows = jnp.transpose(iq, (0, 2, 1, 3)).reshape(Bs, IDX_HEADS * Ts, LANES)
        iw = ga_s[0, :, GLA_RANK:GLA_RANK + IDX_HEADS].reshape(Bs, Ts, IDX_HEADS)
        wb = jnp.broadcast_to(jnp.transpose(iw, (0, 2, 1)).reshape(Bs, IDX_HEADS * Ts, 1), (Bs, IDX_HEADS * Ts, LANES))
        ik_new = ik_s.reshape(Bs, Ts, LANES)
        k_new = k_s.reshape(Bs, Ts, aw)
        v_new = v_s.reshape(Bs, Ts, aw)
        aq = q3[0, :, col3["aq"]:col3["aq"] + aw].reshape(Bs, Ts, aw)
        az = q1[0, :, col1["az"]:col1["az"] + aw].reshape(Bs, Ts, aw)
        scores = _sample_scores(page_table, iq_rows, wb, cache_kidx, l, score_pages)
        o_b_s = _sample_attn(page_table, scores, iq_rows, wb, ik_new, aq, k_new, v_new, az, cache_k, cache_v,
                             l, topk_s, attn_pages)
        res = _outproj(o_a_s, o_b_s.reshape(1, n_s, aw), q1, col1, xs, mod_s[2], wpa, wpb, wo, l, g_next,
                       None if final else nxt_s[:2], n_s)
        xs, hs = (res, None) if final else res
        outs["ks"].append(k_new.reshape(Bs, Ts, ATT_HEADS, LANES))
        outs["vs"].append(v_new.reshape(Bs, Ts, ATT_HEADS, LANES))
        outs["iks"].append(ik_new)
        outs["ss"].append(s_new_s)
        mod_p, mod_s = nxt_p, nxt_s

    k_p, v_p, ik_p = stacked_p
    st = lambda n: jnp.stack(outs[n])
    return (xp, xs.reshape(Bs, Ts, D), k_p.reshape(depth, B, T, ATT_HEADS, LANES),
            v_p.reshape(depth, B, T, ATT_HEADS, LANES), ik_p, st("sp"), st("ks"), st("vs"), st("iks"), st("ss"))
```

```python
import functools

import numpy as np
import jax
import jax.numpy as jnp
from jax import lax
from jax.experimental import pallas as pl
from jax.experimental.pallas import tpu as pltpu

F32 = jnp.float32
BF16 = jnp.bfloat16
I32 = jnp.int32

EPS = 1e-6
PAGE_SIZE = 128
GLA_HEADS = 4
GLA_RANK = 16
GLA_TAU = 16.0
GLA_CHUNK = 64
ATT_HEADS = 8
IDX_HEADS = 16
TOPK_MAX = 256

LANES = 128
SUBLANES = 8
VMEM_LIMIT_BYTES = 56 * 1024 * 1024

NEG_BIG = -1e30
INT_MIN = -2147483648
KEY_NEG_INF = -2139095041


def _cparams(sem):
    return pltpu.CompilerParams(dimension_semantics=sem, vmem_limit_bytes=VMEM_LIMIT_BYTES)


def _dot(a, b):
    return jnp.dot(a, b, preferred_element_type=F32)


def _dot_nt(a, b):
    return lax.dot_general(a, b, (((1,), (1,)), ((), ())), preferred_element_type=F32)


def _sigmoid(x):
    return 1.0 / (1.0 + jnp.exp(-x))


def _sort_key(x):
    bits = pltpu.bitcast(x + 0.0, I32)
    return jnp.where(bits >= 0, bits, bits ^ jnp.int32(0x7FFFFFFF))


def _mod_kernel(c_ref, w_ref, b_ref, o_ref):
    o_ref[0] = _dot(c_ref[...].astype(BF16), w_ref[0].astype(BF16)) + b_ref[0]


def _modulation(c_all, w_c, b_c):
    depth, d, n = w_c.shape
    rows = c_all.shape[0]
    tn = 768
    return pl.pallas_call(
        _mod_kernel,
        grid=(depth, n // tn),
        in_specs=[pl.BlockSpec((rows, d), lambda l, j: (0, 0)),
                  pl.BlockSpec((1, d, tn), lambda l, j: (l, 0, j)),
                  pl.BlockSpec((1, 1, tn), lambda l, j: (l, 0, j))],
        out_specs=pl.BlockSpec((1, rows, tn), lambda l, j: (l, 0, j)),
        out_shape=jax.ShapeDtypeStruct((depth, rows, n), F32),
        compiler_params=_cparams(("parallel", "parallel")),
        name="adaln_modulation",
    )(c_all, w_c, b_c.reshape(depth, 1, n))


def _adaln_norm(x, g, scale, shift):
    y = x * lax.rsqrt(jnp.mean(x * x, axis=-1, keepdims=True) + EPS) * g
    return y * (1.0 + scale) + shift


def _row_spec(a, tm, ngrid):
    rb = 1 if a.shape[1] == 1 else tm
    if ngrid == 2:
        return pl.BlockSpec((1, rb, a.shape[2]), (lambda b, i: (b, 0, 0)) if rb == 1 else (lambda b, i: (b, i, 0)))
    return pl.BlockSpec((1, rb, a.shape[2]), (lambda b, i, j: (b, 0, 0)) if rb == 1 else (lambda b, i, j: (b, i, 0)))


def _prenorm_kernel(x_ref, sc_ref, sh_ref, g_ref, o_ref):
    o_ref[0] = _adaln_norm(x_ref[0], g_ref[...], sc_ref[0], sh_ref[0]).astype(o_ref.dtype)


def _prenorm(x, scale, shift, g, tm):
    B, T, D = x.shape
    return pl.pallas_call(
        _prenorm_kernel,
        grid=(B, T // tm),
        in_specs=[pl.BlockSpec((1, tm, D), lambda b, i: (b, i, 0)), _row_spec(scale, tm, 2), _row_spec(shift, tm, 2),
                  pl.BlockSpec((1, D), lambda b, i: (0, 0))],
        out_specs=pl.BlockSpec((1, tm, D), lambda b, i: (b, i, 0)),
        out_shape=jax.ShapeDtypeStruct((B, T, D), BF16),
        compiler_params=_cparams(("parallel", "parallel")),
        name="prenorm",
    )(x, scale, shift, g.reshape(1, D))


def _inproj_kernel(h_ref, w_ref, o_ref):
    o_ref[0] = _dot(h_ref[0], w_ref[...]).astype(o_ref.dtype)


def _inproj(h, w_all, layer, out_dtype, tm, tn, name):
    B, T, D = h.shape
    N = w_all.shape[2]
    return pl.pallas_call(
        _inproj_kernel,
        grid=(N // tn, B, T // tm),
        in_specs=[pl.BlockSpec((1, tm, D), lambda j, b, i: (b, i, 0)),
                  pl.BlockSpec((None, D, tn), lambda j, b, i: (layer, 0, j))],
        out_specs=pl.BlockSpec((1, tm, tn), lambda j, b, i: (b, i, j)),
        out_shape=jax.ShapeDtypeStruct((B, T, N), out_dtype),
        compiler_params=_cparams(("parallel", "parallel", "parallel")),
        name=name,
    )(h, w_all)


def _inproj_kv_kernel(h_ref, w_ref, *refs, aw, n_alias):
    k32, ik32, v32, ga32, kb, ikb, vb = refs[n_alias:]
    r = _dot(h_ref[0], w_ref[...])
    t = aw + LANES
    k32[0] = r[:, :aw]
    ik32[0] = r[:, aw:t]
    v32[0] = r[:, t:t + aw]
    ga32[0] = r[:, t + aw:]
    kb[0] = r[:, :aw].astype(BF16)
    ikb[0] = r[:, aw:t].astype(BF16)
    vb[0] = r[:, t:t + aw].astype(BF16)


def _inproj_kv(h, w_all, layer, stacked, depth_out, layer_out, tm):
    B, T, D = h.shape
    aw = ATT_HEADS * LANES
    stacked = () if stacked is None else tuple(stacked)
    any_spec = pl.BlockSpec(memory_space=pl.ANY)
    lay = lambda w: pl.BlockSpec((None, 1, tm, w), lambda b, i: (layer_out, b, i, 0))
    per = lambda w: pl.BlockSpec((1, tm, w), lambda b, i: (b, i, 0))
    f32s = lambda w: jax.ShapeDtypeStruct((depth_out, B, T, w), F32)
    outs = pl.pallas_call(
        functools.partial(_inproj_kv_kernel, aw=aw, n_alias=len(stacked)),
        grid=(B, T // tm),
        in_specs=[pl.BlockSpec((1, tm, D), lambda b, i: (b, i, 0)),
                  pl.BlockSpec((None, D, w_all.shape[2]), lambda b, i: (layer, 0, 0),
                               pipeline_mode=pl.Buffered(1))] + [any_spec] * len(stacked),
        out_specs=[lay(aw), lay(LANES), lay(aw), per(LANES), per(aw), per(LANES), per(aw)],
        out_shape=[f32s(aw), f32s(LANES), f32s(aw), jax.ShapeDtypeStruct((B, T, LANES), F32),
                   jax.ShapeDtypeStruct((B, T, aw), BF16), jax.ShapeDtypeStruct((B, T, LANES), BF16),
                   jax.ShapeDtypeStruct((B, T, aw), BF16)],
        input_output_aliases={2: 0, 3: 2, 4: 1} if stacked else {},
        compiler_params=_cparams(("parallel", "parallel")),
        name="inproj_kv",
    )(h, w_all, *stacked)
    k32, ik32, v32, ga32, kb, ikb, vb = outs
    return (k32, v32, ik32), ga32, kb, ikb, vb


def _gla_constants(C):
    r = np.arange(C)[:, None]
    j = np.arange(C)[None, :]
    blocks = [(j <= r), (j > r)]
    masks = []
    w = C // 2
    while w >= 1:
        mid = (r // (2 * w)) * (2 * w) + w
        upper = r >= mid
        blocks.append(np.where(upper, (j >= mid) & (j <= r), (j > r) & (j < mid)))
        t, s = r, j
        same = (t // (2 * w)) == (s // (2 * w))
        t_up = t >= (t // (2 * w)) * (2 * w) + w
        s_lo = s < (s // (2 * w)) * (2 * w) + w
        masks.append(same & t_up & s_lo)
        w //= 2
    masks.append(r == j)
    W = np.concatenate(blocks, axis=0).astype(np.float32)
    W = np.concatenate([W, W, W], axis=1)
    M = np.stack(masks).astype(np.float32)
    return jnp.asarray(W, dtype=BF16), jnp.asarray(M, dtype=F32)


def _gla_kernel(q_ref, k_ref, v_ref, z_ref, ga_ref, wa_ref, ba_ref, gg_ref, s0_ref, W_ref, M_ref,
                o_ref, sout_ref, S_ref, *, C, nchunk, nlev, valid, H, dk, dv):
    ci = pl.program_id(1)

    @pl.when(ci == 0)
    def _():
        S_ref[...] = s0_ref[0]

    W = W_ref[...]
    for c in range(nchunk):
        rows = pl.ds(c * C, C)
        zz = _dot(ga_ref[0, rows, :].astype(BF16), wa_ref[...]) + ba_ref[...]
        g = -(jnp.maximum(-zz, 0.0) + jnp.log1p(jnp.exp(-jnp.abs(zz)))) * (1.0 / GLA_TAU)
        if valid is not None:
            tok = ci * (nchunk * C) + c * C + lax.broadcasted_iota(I32, g.shape, 0)
            g = jnp.where(tok < valid, g, 0.0)
        g1 = g.astype(BF16)
        r1 = g - g1.astype(F32)
        g2 = r1.astype(BF16)
        g3 = (r1 - g2.astype(F32)).astype(BF16)
        E_all = jnp.exp(_dot(W, jnp.concatenate([g1, g2, g3], axis=0)))
        for h in range(H):
            kc = slice(h * dk, (h + 1) * dk)
            vc = slice(h * dv, (h + 1) * dv)
            E = E_all[:, kc]
            qs = q_ref[0, rows, kc] * (dk ** -0.5)
            k = k_ref[0, rows, kc]
            v = v_ref[0, rows, vc].astype(BF16)
            e_b = E[0:C]
            e_rest = E[C:2 * C]
            A = jnp.where(M_ref[nlev] > 0, _dot_nt(qs.astype(BF16), k.astype(BF16)), 0.0)
            for l in range(nlev):
                e_l = E[(2 + l) * C:(3 + l) * C]
                A = A + jnp.where(M_ref[l] > 0, _dot_nt((qs * e_l).astype(BF16), (k * e_l).astype(BF16)), 0.0)
            S = S_ref[h]
            o = _dot(A.astype(BF16), v) + _dot((qs * e_b).astype(BF16), S.astype(BF16))
            k_dec = k * e_rest
            e_last = e_b.T[:, C - 1:C]
            S_ref[h] = e_last * S + _dot(k_dec.T.astype(BF16), v)
            y = o * lax.rsqrt(jnp.mean(o * o, axis=-1, keepdims=True) + EPS) * gg_ref[h:h + 1, :]
            zg = z_ref[0, rows, vc]
            o_ref[0, rows, vc] = (y * (zg * _sigmoid(zg))).astype(o_ref.dtype)

    @pl.when(ci == pl.num_programs(1) - 1)
    def _():
        sout_ref[0] = S_ref[...]


def _gla(p1, p2, col, s0, wa, ba, gg, consts, tc, valid):
    B, T, _ = p1.shape
    H, dk, dv = s0.shape[1], s0.shape[2], s0.shape[3]
    W, M = consts
    C = GLA_CHUNK
    nlev = M.shape[0] - 1
    qo, ko, vo, zo = (col[n] // (H * w) for n, w in (("gq", dk), ("gk", dk), ("gv", dv), ("gz", dv)))
    gao = col["gaiw"] // LANES
    kern = functools.partial(_gla_kernel, C=C, nchunk=tc // C, nlev=nlev, valid=valid, H=H, dk=dk, dv=dv)
    return pl.pallas_call(
        kern,
        grid=(B, T // tc),
        in_specs=[pl.BlockSpec((1, tc, H * dk), lambda b, i: (b, i, qo)),
                  pl.BlockSpec((1, tc, H * dk), lambda b, i: (b, i, ko)),
                  pl.BlockSpec((1, tc, H * dv), lambda b, i: (b, i, vo)),
                  pl.BlockSpec((1, tc, H * dv), lambda b, i: (b, i, zo)),
                  pl.BlockSpec((1, tc, LANES), lambda b, i: (b, i, gao)),
                  pl.BlockSpec((LANES, H * dk), lambda b, i: (0, 0)),
                  pl.BlockSpec((1, H * dk), lambda b, i: (0, 0)),
                  pl.BlockSpec((H, dv), lambda b, i: (0, 0)),
                  pl.BlockSpec((1, H, dk, dv), lambda b, i: (b, 0, 0, 0)),
                  pl.BlockSpec(W.shape, lambda b, i: (0, 0)),
                  pl.BlockSpec(M.shape, lambda b, i: (0, 0, 0))],
        out_specs=[pl.BlockSpec((1, tc, H * dv), lambda b, i: (b, i, 0)),
                   pl.BlockSpec((1, H, dk, dv), lambda b, i: (b, 0, 0, 0))],
        out_shape=[jax.ShapeDtypeStruct((B, T, H * dv), BF16),
                   jax.ShapeDtypeStruct((B, H, dk, dv), F32)],
        scratch_shapes=[pltpu.VMEM((H, dk, dv), F32)],
        compiler_params=_cparams(("parallel", "arbitrary")),
        name="gla_branch",
    )(p1, p1, p1, p1, p2, wa, ba, gg, s0, W, M)


def _kth_key(count_ge, kk, rows):
    def body(i, u):
        cand = u | (jnp.int32(1) << (31 - i))
        ok = count_ge(cand ^ jnp.int32(INT_MIN)) >= kk
        return jnp.where(ok, cand, u)
    u = lax.fori_loop(0, 32, body, jnp.zeros((rows, 1), I32))
    return u ^ jnp.int32(INT_MIN)


def _kth_key_groups(keys, kk):
    def body(i, us):
        bit = jnp.int32(1) << (31 - i)
        cands = [u | bit for u in us]
        oks = [_count(key >= (c ^ jnp.int32(INT_MIN))) >= kk for key, c in zip(keys, cands)]
        return tuple(jnp.where(ok, c, u) for ok, c, u in zip(oks, cands, us))
    us = lax.fori_loop(0, 32, body, tuple(jnp.zeros((k.shape[0], 1), I32) for k in keys), unroll=2)
    return jnp.concatenate([u ^ jnp.int32(INT_MIN) for u in us], axis=0)


def _tie_cutoff(count_eq_below, need, rows, nbits):
    def body(i, c):
        cand = c | (jnp.int32(1) << (nbits - 1 - i))
        ok = count_eq_below(cand) <= need
        return jnp.where(ok, cand, c)
    return lax.fori_loop(0, nbits, body, jnp.zeros((rows, 1), I32))


def _count(mask):
    return jnp.sum(jnp.where(mask, 1.0, 0.0), axis=1, keepdims=True)


def _dsa_block(slope_ref, iq_ref, ik_ref, iw_ref, q_ref, k_ref, v_ref, z_ref, o_ref, bias_ref, *, S, tq, topk, iw_off,
               scale, idx_scale):
    q0 = pl.program_id(1) * tq
    pos = q0 + lax.broadcasted_iota(I32, (tq, S), 0)
    kpos = lax.broadcasted_iota(I32, (tq, S), 1)
    head_cols = lambda h: pl.ds(pl.multiple_of(h * LANES, LANES), LANES)

    if S <= topk:
        bias_ref[:, 0:S] = jnp.where(kpos <= pos, 0.0, NEG_BIG)
    else:
        iw = iw_ref[0]
        ik = ik_ref[0, 0:S, :]
        score = jnp.zeros((tq, S), F32)
        for h in range(IDX_HEADS):
            s = _dot_nt(iq_ref[0, :, h * LANES:(h + 1) * LANES], ik)
            score = score + jnp.maximum(s, 0.0) * (iw[:, iw_off + h:iw_off + h + 1] * idx_scale)
        key = _sort_key(jnp.where(kpos <= pos, score, -jnp.inf))

        kk = float(topk)
        ng = 2
        thr = _kth_key_groups([key[i * tq // ng:(i + 1) * tq // ng] for i in range(ng)], kk)
        thr_sel = jnp.maximum(thr, KEY_NEG_INF + 1)
        bias_ref[:, 0:S] = jnp.where(key >= thr_sel, 0.0, NEG_BIG)
        tied = jnp.where(_count(key >= thr_sel) > kk, 1.0, 0.0)

        @pl.when(jnp.max(tied) > 0.0)
        def _():
            need = kk - _count(key > thr)
            eq_pos = jnp.where(key == thr, kpos, jnp.int32(2 ** 30))
            cut = _tie_cutoff(lambda c_: _count(eq_pos < c_), need, tq, int(S).bit_length())
            keep_eq = jnp.where(key == thr, jnp.where(kpos < cut, 0.0, NEG_BIG), NEG_BIG)
            bias_ref[:, 0:S] = jnp.where(key >= thr_sel, jnp.where(key > thr, 0.0, keep_eq), NEG_BIG)

    log2e = 1.4426950408889634
    krel = (lax.broadcasted_iota(I32, (1, S), 1) - q0).astype(F32)

    def attend(h, carry):
        cols = head_cols(h)
        qk = _dot_nt(q_ref[0, :, cols], k_ref[0, 0:S, cols])
        logits = qk * (scale * log2e) + krel * (slope_ref[h] * log2e) + bias_ref[:, 0:S]
        m = jnp.max(logits, axis=1, keepdims=True)
        p = jnp.exp2(logits - m)
        l = jnp.sum(p, axis=1, keepdims=True)
        out = _dot(p.astype(BF16), v_ref[0, 0:S, cols]) / l
        zg = z_ref[0, :, cols]
        o_ref[0, :, cols] = (out * (zg * _sigmoid(zg))).astype(o_ref.dtype)
        return carry

    lax.fori_loop(0, ATT_HEADS, attend, 0, unroll=4)


def _key_extents(T, tq, topk):
    marks = sorted({min(T, max(tq, topk)), T // 4, 3 * T // 8, T // 2, 3 * T // 4, T})
    return [m for m in marks if m % tq == 0 and m >= tq]


def _dsa_kernel(*refs, tq, T, extents, **kw):
    last = (pl.program_id(1) + 1) * tq
    lo = 0
    for S in extents:
        pl.when((last > lo) & (last <= S))(functools.partial(_dsa_block, *refs, S=S, tq=tq, **kw))
        lo = S


def _dsa_prompt(p1, p3, ga32, ikb, kb, vb, col1, col3, topk, tq):
    B, T, _ = p1.shape
    aw = ATT_HEADS * LANES
    kern = functools.partial(_dsa_kernel, tq=tq, T=T, extents=_key_extents(T, tq, topk), topk=topk, iw_off=GLA_RANK,
                             scale=LANES ** -0.5, idx_scale=(LANES * IDX_HEADS) ** -0.5)
    iq_o = col3["iq"] // (IDX_HEADS * LANES)
    slopes = jnp.asarray(2.0 ** -(np.arange(ATT_HEADS) + 1.0), F32)
    return pl.pallas_call(
        kern,
        grid=(B, T // tq),
        in_specs=[pl.BlockSpec(memory_space=pltpu.SMEM),
                  pl.BlockSpec((1, tq, IDX_HEADS * LANES), lambda b, i: (b, i, iq_o)),
                  pl.BlockSpec((1, T, LANES), lambda b, i: (b, 0, 0)),
                  pl.BlockSpec((1, tq, LANES), lambda b, i: (b, i, 0)),
                  pl.BlockSpec((1, tq, aw), lambda b, i: (b, i, col3["aq"] // aw)),
                  pl.BlockSpec((1, T, aw), lambda b, i: (b, 0, 0)),
                  pl.BlockSpec((1, T, aw), lambda b, i: (b, 0, 0)),
                  pl.BlockSpec((1, tq, aw), lambda b, i: (b, i, col1["az"] // aw))],
        out_specs=pl.BlockSpec((1, tq, aw), lambda b, i: (b, i, 0)),
        out_shape=jax.ShapeDtypeStruct((B, T, aw), BF16),
        scratch_shapes=[pltpu.VMEM((tq, T), F32)],
        compiler_params=_cparams(("parallel", "arbitrary")),
        name="dsa_prompt",
    )(slopes, p3, ikb, ga32, p3, kb, vb, p1)


def _page_specs(n, block, layer, pages_per_step):
    def spec(g):
        return pl.BlockSpec(block, lambda b, p, pt: (layer, pt[b, p * pages_per_step + g]) + (0,) * (len(block) - 2))
    return [spec(g) for g in range(n)]


def _sample_scores_kernel(pt_ref, iq_ref, wb_ref, *rest, idx_scale):
    kidx_refs, o_ref = rest[:-1], rest[-1]
    iq = iq_ref[0]
    w = wb_ref[0] * idx_scale
    tq = o_ref.shape[1]
    for g, kidx_ref in enumerate(kidx_refs):
        s = jnp.maximum(_dot_nt(iq, kidx_ref[...].astype(BF16)), 0.0) * w
        o_ref[0, :, g * PAGE_SIZE:(g + 1) * PAGE_SIZE] = jnp.sum(s.reshape(IDX_HEADS, tq, PAGE_SIZE), axis=0)


def _sample_scores(page_table, iq_rows, wb, cache_kidx, layer, pages_per_step):
    B, n_pages = page_table.shape
    tq = iq_rows.shape[1] // IDX_HEADS
    G = pages_per_step
    kern = functools.partial(_sample_scores_kernel, idx_scale=(LANES * IDX_HEADS) ** -0.5)
    return pl.pallas_call(
        kern,
        grid_spec=pltpu.PrefetchScalarGridSpec(
            num_scalar_prefetch=1,
            grid=(B, n_pages // G),
            in_specs=[pl.BlockSpec((1, IDX_HEADS * tq, LANES), lambda b, p, pt: (b, 0, 0)),
                      pl.BlockSpec((1, IDX_HEADS * tq, LANES), lambda b, p, pt: (b, 0, 0))]
                     + _page_specs(G, (None, None, PAGE_SIZE, LANES), layer, G),
            out_specs=pl.BlockSpec((1, tq, G * PAGE_SIZE), lambda b, p, pt: (b, 0, p)),
        ),
        out_shape=jax.ShapeDtypeStruct((B, tq, n_pages * PAGE_SIZE), F32),
        compiler_params=_cparams(("parallel", "arbitrary")),
        name="sample_scores",
    )(page_table, iq_rows, wb, *([cache_kidx] * G))


def _sample_attn_kernel(pt_ref, sc_ref, iq_ref, wb_ref, ikn_ref, q_ref, kn_ref, vn_ref, z_ref, *rest,
                        G, tq, past, topk, scale, idx_scale):
    ck_refs, cv_refs, o_ref = rest[:G], rest[G:2 * G], rest[2 * G]
    qs_ref, thr_ref, cut_ref, bnew_ref, m_ref, l_ref, acc_ref = rest[2 * G + 1:]
    p = pl.program_id(1)
    H = ATT_HEADS
    lane = lax.broadcasted_iota(I32, (tq, LANES), 1)
    row = lax.broadcasted_iota(I32, (tq, LANES), 0)

    @pl.when(p == 0)
    def _():
        q = q_ref[0].astype(F32)
        qs_ref[...] = jnp.concatenate([q[:, h * LANES:(h + 1) * LANES] for h in range(H)], axis=0)
        ikn = jnp.concatenate([ikn_ref[0], jnp.zeros((LANES - tq, LANES), F32)], axis=0).astype(BF16)
        s = jnp.maximum(_dot_nt(iq_ref[0], ikn), 0.0) * (wb_ref[0] * idx_scale)
        s_new = jnp.sum(s.reshape(IDX_HEADS, tq, LANES), axis=0)
        key_new = _sort_key(jnp.where((lane <= row) & (lane < tq), s_new, -jnp.inf))
        key_past = _sort_key(sc_ref[0])

        kk = float(topk)
        count_ge = lambda t: _count(key_past >= t) + _count(key_new >= t)
        thr = _kth_key(count_ge, kk, tq)
        thr_ref[...] = jnp.broadcast_to(thr, thr_ref.shape)
        cut_ref[...] = jnp.full(cut_ref.shape, past + LANES, I32)
        tied = jnp.where((count_ge(thr) > kk) & (thr > KEY_NEG_INF), 1.0, 0.0)

        @pl.when(jnp.max(tied) > 0.0)
        def _():
            need = kk - _count(key_past > thr) - _count(key_new > thr)
            idx_past = lax.broadcasted_iota(I32, key_past.shape, 1)
            eq_past = key_past == thr
            eq_new = key_new == thr
            cnt = lambda c_: _count(eq_past & (idx_past < c_)) + _count(eq_new & (lane + past < c_))
            c = _tie_cutoff(cnt, need, tq, int(past + LANES).bit_length())
            cut_ref[...] = jnp.broadcast_to(c, cut_ref.shape)

        sel_new = (key_new > thr) | ((key_new == thr) & (lane + past < cut_ref[...]))
        bnew_ref[...] = jnp.where(sel_new & (lane <= row) & (lane < tq), 0.0, NEG_BIG)
        m_ref[...] = jnp.full(m_ref.shape, -jnp.inf, F32)
        l_ref[...] = jnp.zeros(l_ref.shape, F32)
        acc_ref[...] = jnp.zeros(acc_ref.shape, F32)

    thr = thr_ref[...]
    cut = cut_ref[...]
    bias, dist = [], []
    for g in range(G):
        first = pl.multiple_of((p * G + g) * PAGE_SIZE, PAGE_SIZE)
        key_p = _sort_key(sc_ref[0, :, pl.ds(first, PAGE_SIZE)])
        kpos = lane + first
        keep_eq = jnp.where(key_p == thr, jnp.where(kpos < cut, 0.0, NEG_BIG), NEG_BIG)
        bias.append(jnp.where(key_p > thr, 0.0, keep_eq))
        dist.append((row + past - kpos).astype(F32))
    head_rows = [pl.ds(h, PAGE_SIZE, stride=H) for h in range(H)]
    qs = qs_ref[...]
    logits = []
    for h in range(H):
        q_h = qs[h * tq:(h + 1) * tq].astype(BF16)
        logits.append([_dot_nt(q_h, ck_refs[g][head_rows[h], :].astype(BF16)) * scale
                       - (2.0 ** -(h + 1)) * dist[g] + bias[g] for g in range(G)])
    m_old = m_ref[:, 0:1]
    page_max = jnp.concatenate([functools.reduce(jnp.maximum, lg) for lg in logits], axis=0)
    m_new = jnp.maximum(m_old, jnp.max(page_max, axis=1, keepdims=True))
    alpha = jnp.exp(m_old - m_new)
    pe = [[jnp.exp(logits[h][g] - m_new[h * tq:(h + 1) * tq]) for g in range(G)] for h in range(H)]
    psum = jnp.concatenate([functools.reduce(jnp.add, ph) for ph in pe], axis=0)
    pv = [functools.reduce(jnp.add, [_dot(pe[h][g].astype(BF16), cv_refs[g][head_rows[h], :].astype(BF16))
                                     for g in range(G)]) for h in range(H)]
    l_new = alpha * l_ref[:, 0:1] + jnp.sum(psum, axis=1, keepdims=True)
    acc_ref[...] = alpha * acc_ref[...] + jnp.concatenate(pv, axis=0)
    l_ref[...] = jnp.broadcast_to(l_new, l_ref.shape)
    m_ref[...] = jnp.broadcast_to(m_new, m_ref.shape)

    @pl.when(p == pl.num_programs(1) - 1)
    def _():
        zpad = jnp.zeros((LANES - tq, LANES), F32)
        bnew = bnew_ref[...]
        dist = (row - lane).astype(F32)
        qs = qs_ref[...].astype(BF16)
        lg = []
        for h in range(H):
            cols = slice(h * LANES, (h + 1) * LANES)
            kn = jnp.concatenate([kn_ref[0, :, cols], zpad], axis=0).astype(BF16)
            qk = _dot_nt(qs, kn)[h * tq:(h + 1) * tq]
            lg.append(qk * scale - (2.0 ** -(h + 1)) * dist + bnew)
        logits_new = jnp.concatenate(lg, axis=0)
        m_old = m_ref[:, 0:1]
        m_new = jnp.maximum(m_old, jnp.max(logits_new, axis=1, keepdims=True))
        alpha = jnp.exp(m_old - m_new)
        pe = jnp.exp(logits_new - m_new)
        l_fin = alpha * l_ref[:, 0:1] + jnp.sum(pe, axis=1, keepdims=True)
        acc = alpha * acc_ref[...]
        pe = pe.astype(BF16)
        for h in range(H):
            cols = slice(h * LANES, (h + 1) * LANES)
            vn = jnp.concatenate([vn_ref[0, :, cols], zpad], axis=0).astype(BF16)
            rows = slice(h * tq, (h + 1) * tq)
            out = (acc[rows] + _dot(pe, vn)[rows]) / l_fin[rows]
            zg = z_ref[0, :, cols]
            o_ref[0, :, cols] = (out * (zg * _sigmoid(zg))).astype(o_ref.dtype)


def _sample_attn(page_table, scores, iq_rows, wb, ik_new, q, k_new, v_new, z, cache_k, cache_v, layer, topk,
                 pages_per_step):
    B, n_pages = page_table.shape
    tq = q.shape[1]
    H = ATT_HEADS
    aw = H * LANES
    past = n_pages * PAGE_SIZE
    G = pages_per_step
    kern = functools.partial(_sample_attn_kernel, G=G, tq=tq, past=past, topk=topk, scale=LANES ** -0.5,
                             idx_scale=(LANES * IDX_HEADS) ** -0.5)
    per_b = lambda b, p, pt: (b, 0, 0)
    page_block = (None, None, PAGE_SIZE * H, LANES)
    cache_k, cache_v = (c.reshape(c.shape[0], c.shape[1], PAGE_SIZE * H, LANES) for c in (cache_k, cache_v))
    return pl.pallas_call(
        kern,
        grid_spec=pltpu.PrefetchScalarGridSpec(
            num_scalar_prefetch=1,
            grid=(B, n_pages // G),
            in_specs=[pl.BlockSpec((1, tq, past), per_b),
                      pl.BlockSpec((1, IDX_HEADS * tq, LANES), per_b),
                      pl.BlockSpec((1, IDX_HEADS * tq, LANES), per_b),
                      pl.BlockSpec((1, tq, LANES), per_b),
                      pl.BlockSpec((1, tq, aw), per_b),
                      pl.BlockSpec((1, tq, aw), per_b),
                      pl.BlockSpec((1, tq, aw), per_b),
                      pl.BlockSpec((1, tq, aw), per_b)]
                     + _page_specs(G, page_block, layer, G) + _page_specs(G, page_block, layer, G),
            out_specs=pl.BlockSpec((1, tq, aw), per_b),
            scratch_shapes=[pltpu.VMEM((H * tq, LANES), F32),
                            pltpu.VMEM((tq, LANES), I32),
                            pltpu.VMEM((tq, LANES), I32),
                            pltpu.VMEM((tq, LANES), F32),
                            pltpu.VMEM((H * tq, LANES), F32),
                            pltpu.VMEM((H * tq, LANES), F32),
                            pltpu.VMEM((H * tq, LANES), F32)],
        ),
        out_shape=jax.ShapeDtypeStruct((B, tq, aw), BF16),
        compiler_params=_cparams(("parallel", "arbitrary")),
        name="sample_attention",
    )(page_table, scores, iq_rows, wb, ik_new, q, k_new, v_new, z, *([cache_k] * G), *([cache_v] * G))


def _outproj_kernel(oa_ref, ob_ref, ma_ref, mb_ref, x_ref, gate_ref, wpa_ref, wpb_ref, wo_ref, g_ref, *rest, final):
    pa = _dot(oa_ref[0], wpa_ref[...])
    pb = _dot(ob_ref[0], wpb_ref[...])
    merged = _sigmoid(ma_ref[0]) * pa + _sigmoid(mb_ref[0]) * pb
    x = x_ref[0] + gate_ref[0] * _dot(merged.astype(BF16), wo_ref[...])
    if final:
        (o_ref,) = rest
        o_ref[0] = x * lax.rsqrt(jnp.mean(x * x, axis=-1, keepdims=True) + EPS) * g_ref[...]
    else:
        sc_ref, sh_ref, o_ref, h_ref = rest
        o_ref[0] = x
        h_ref[0] = _adaln_norm(x, g_ref[...], sc_ref[0], sh_ref[0]).astype(h_ref.dtype)


def _outproj(o_a, o_b, p1, col1, x, gate, w_pa, w_pb, w_out, layer, g, nxt, tm):
    B, T, D = x.shape
    final = nxt is None
    const = lambda b, i: (0, 0)
    wspec = lambda w: pl.BlockSpec((None,) + w.shape[1:], lambda b, i: (layer, 0, 0), pipeline_mode=pl.Buffered(1))
    xspec = pl.BlockSpec((1, tm, D), lambda b, i: (b, i, 0))
    extra = [] if final else [_row_spec(nxt[0], tm, 2), _row_spec(nxt[1], tm, 2)]
    return pl.pallas_call(
        functools.partial(_outproj_kernel, final=final),
        grid=(B, T // tm),
        in_specs=[pl.BlockSpec((1, tm, o_a.shape[2]), lambda b, i: (b, i, 0)),
                  pl.BlockSpec((1, tm, o_b.shape[2]), lambda b, i: (b, i, 0)),
                  pl.BlockSpec((1, tm, D), lambda b, i: (b, i, col1["m_a"] // D)),
                  pl.BlockSpec((1, tm, D), lambda b, i: (b, i, col1["m_b"] // D)),
                  xspec, _row_spec(gate, tm, 2), wspec(w_pa), wspec(w_pb), wspec(w_out),
                  pl.BlockSpec((1, D), const)] + extra,
        out_specs=xspec if final else [xspec, xspec],
        out_shape=(jax.ShapeDtypeStruct((B, T, D), F32) if final else
                   [jax.ShapeDtypeStruct((B, T, D), F32), jax.ShapeDtypeStruct((B, T, D), BF16)]),
        compiler_params=_cparams(("parallel", "parallel")),
        name="merge_outproj",
    )(o_a, o_b, p1, p1, x, gate, w_pa, w_pb, w_out, g.reshape(1, D), *([] if final else nxt))


def _column_groups(d):
    gqk, gv, att, iqw = d // 4, d // 2, d // 2, d
    sizes = [("gq", gqk), ("gk", gqk), ("gv", gv), ("gz", gv), ("ga", GLA_RANK), ("aq", att), ("ak", att),
             ("av", att), ("az", att), ("iq", iqw), ("ik", LANES), ("iw", IDX_HEADS), ("m_a", d), ("m_b", d)]
    src, o = {}, 0
    for n, s in sizes:
        src[n] = (o, o + s)
        o += s
    groups = (("gq", "gk", "gv", "gz", "az", "m_a", "m_b"), ("ak", "ik", "av", "gaiw"), ("iq", "aq"))
    return src, groups


def _group_layout(src, names):
    offs, o = {}, 0
    for n in names:
        offs[n] = o
        o += LANES if n == "gaiw" else src[n][1] - src[n][0]
    return offs, o


def _regroup_kernel(wt_ref, *out_refs, src, groups):
    rows = 4 * LANES
    for names, o_ref in zip(groups, out_refs):
        offs, _ = _group_layout(src, names)
        for n in names:
            if n == "gaiw":
                pad = jnp.zeros((LANES - GLA_RANK - IDX_HEADS, LANES), F32)
                blk = jnp.concatenate([wt_ref[src["ga"][0]:src["ga"][1], :], wt_ref[src["iw"][0]:src["iw"][1], :],
                                       pad], axis=0)
                o_ref[:, offs[n]:offs[n] + LANES] = blk.T.astype(BF16)
            else:
                a, b = src[n]
                for r in range(a, b, rows):
                    e = min(b, r + rows)
                    o_ref[:, offs[n] + r - a:offs[n] + e - a] = wt_ref[r:e, :].T.astype(BF16)


def _regroup_weights(w, src, groups):
    depth, D, n_in = w.shape
    widths = [_group_layout(src, names)[1] for names in groups]
    return pl.pallas_call(
        functools.partial(_regroup_kernel, src=src, groups=groups),
        grid=(depth, D // LANES),
        in_specs=[pl.BlockSpec((None, n_in, LANES), lambda l, i: (l, 0, i))],
        out_specs=[pl.BlockSpec((None, LANES, n), lambda l, i: (l, i, 0)) for n in widths],
        out_shape=[jax.ShapeDtypeStruct((depth, D, n), BF16) for n in widths],
        compiler_params=_cparams(("parallel", "parallel")),
        name="regroup_weights",
    )(jnp.swapaxes(w, 1, 2))


def kernel(x_prompt, x_sample, cache_k, cache_v, cache_kidx, state_gla, page_table, c_prompt, c_sample, w_c, b_c,
           g_norm, w_in, w_a2, b_a2, g_gla, w_pa, w_pb, w_out, g_final):
    depth = w_in.shape[0]
    B, T, D = x_prompt.shape
    Bs, Ts, _ = x_sample.shape
    H, dk, dv = state_gla.shape[2], state_gla.shape[3], state_gla.shape[4]
    n_pages = page_table.shape[1]
    past = n_pages * PAGE_SIZE
    aw = ATT_HEADS * LANES
    n_s = Bs * Ts
    src, groups = _column_groups(D)

    rows = -(-(B + Bs) // SUBLANES) * SUBLANES
    c_all = jnp.concatenate([c_prompt, c_sample, jnp.zeros((rows - B - Bs, D), F32)], axis=0)
    mod = _modulation(c_all, w_c, b_c)

    w1, w2, w3 = _regroup_weights(w_in, src, groups)
    col1, col3 = _group_layout(src, groups[0])[0], _group_layout(src, groups[2])[0]
    wa = jnp.concatenate([w_a2, jnp.zeros((depth, LANES - GLA_RANK, w_a2.shape[2]), F32)], axis=1).astype(BF16)
    wpa, wpb, wo = w_pa.astype(BF16), w_pb.astype(BF16), w_out.astype(BF16)
    gla_col = {**col1, "gaiw": 0}

    gla_consts = _gla_constants(GLA_CHUNK)
    score_pages = min(16, n_pages)
    attn_pages = min(16, n_pages)
    topk_p = min(TOPK_MAX, T // 4)
    topk_s = min(TOPK_MAX, (past + Ts) // 4)
    ts_pad = GLA_CHUNK
    tm = min(T, 512)

    def modulation(l):
        shift, scale, gate = (mod[l, :, i * D:(i + 1) * D] for i in range(3))
        prompt = tuple(a[:B].reshape(B, 1, D) for a in (scale, shift, gate))
        sample = tuple(jnp.repeat(a[B:B + Bs], Ts, axis=0).reshape(1, n_s, D) for a in (scale, shift, gate))
        return prompt, sample

    xp, xs = x_prompt, x_sample.reshape(1, n_s, D)
    mod_p, mod_s = modulation(0)
    hp = _prenorm(xp, mod_p[0], mod_p[1], g_norm[0], tm)
    hs = _prenorm(xs, mod_s[0], mod_s[1], g_norm[0], n_s)
    stacked_p = None
    outs = {n: [] for n in ("sp", "ks", "vs", "iks", "ss")}
    for l in range(depth):
        final = l == depth - 1
        nxt_p, nxt_s = (None, None) if final else modulation(l + 1)
        g_next = g_final if final else g_norm[l + 1]
        ba = b_a2[l].reshape(1, -1)

        p1 = _inproj(hp, w1, l, F32, tm, 2048, "inproj_gates")
        p3 = _inproj(hp, w3, l, BF16, tm, 1024, "inproj_queries")
        stacked_p, ga32, kb, ikb, vb = _inproj_kv(hp, w2, l, stacked_p, depth, l, tm)
        s0 = jnp.zeros((B, H, dk, dv), F32)
        o_a, s_new = _gla(p1, ga32, gla_col, s0, wa[l], ba, g_gla[l], gla_consts, 128, None)
        o_b = _dsa_prompt(p1, p3, ga32, ikb, kb, vb, col1, col3, topk_p, 128)
        res = _outproj(o_a, o_b, p1, col1, xp, mod_p[2], wpa, wpb, wo, l, g_next,
                       None if final else nxt_p[:2], 256)
        xp, hp = (res, None) if final else res
        outs["sp"].append(s_new)

        q1 = _inproj(hs, w1, l, F32, n_s, 2048, "inproj_gates_s")
        q3 = _inproj(hs, w3, l, BF16, n_s, 1024, "inproj_queries_s")
        (k_s, v_s, ik_s), ga_s, _, _, _ = _inproj_kv(hs, w2, l, None, 1, 0, n_s)
        pad_t = lambda a: jnp.pad(a.reshape(Bs, Ts, -1), ((0, 0), (0, ts_pad - Ts), (0, 0)))
        o_a_s, s_new_s = _gla(pad_t(q1[0, :, :col1["az"]]), pad_t(ga_s[0]), gla_col, state_gla[l], wa[l], ba,
                              g_gla[l], gla_consts, ts_pad, Ts)
        o_a_s = o_a_s[:, :Ts].reshape(1, n_s, -1)

        iq = q3[0, :, col3["iq"]:col3["iq"] + IDX_HEADS * LANES].reshape(Bs, Ts, IDX_HEADS, LANES)
        iq_rows = jnp.transpose(iq, (0, 2, 1, 3)).reshape(Bs, IDX_HEADS * Ts, LANES)
        iw = ga_s[0, :, GLA_RANK:GLA_RANK + IDX_HEADS].reshape(Bs, Ts, IDX_HEADS)
        wb = jnp.broadcast_to(jnp.transpose(iw, (0, 2, 1)).reshape(Bs, IDX_HEADS * Ts, 1), (Bs, IDX_HEADS * Ts, LANES))
        ik_new = ik_s.reshape(Bs, Ts, LANES)
        k_new = k_s.reshape(Bs, Ts, aw)
        v_new = v_s.reshape(Bs, Ts, aw)
        aq = q3[0, :, col3["aq"]:col3["aq"] + aw].reshape(Bs, Ts, aw)
        az = q1[0, :, col1["az"]:col1["az"] + aw].reshape(Bs, Ts, aw)
        scores = _sample_scores(page_table, iq_rows, wb, cache_kidx, l, score_pages)
        o_b_s = _sample_attn(page_table, scores, iq_rows, wb, ik_new, aq, k_new, v_new, az, cache_k, cache_v,
                             l, topk_s, attn_pages)
        res = _outproj(o_a_s, o_b_s.reshape(1, n_s, aw), q1, col1, xs, mod_s[2], wpa, wpb, wo, l, g_next,
                       None if final else nxt_s[:2], n_s)
        xs, hs = (res, None) if final else res
        outs["ks"].append(k_new.reshape(Bs, Ts, ATT_HEADS, LANES))
        outs["vs"].append(v_new.reshape(Bs, Ts, ATT_HEADS, LANES))
        outs["iks"].append(ik_new)
        outs["ss"].append(s_new_s)
        mod_p, mod_s = nxt_p, nxt_s

    k_p, v_p, ik_p = stacked_p
    st = lambda n: jnp.stack(outs[n])
    return (xp, xs.reshape(Bs, Ts, D), k_p.reshape(depth, B, T, ATT_HEADS, LANES),
            v_p.reshape(depth, B, T, ATT_HEADS, LANES), ik_p, st("sp"), st("ks"), st("vs"), st("iks"), st("ss"))
```

```python
import functools

import numpy as np
import jax
import jax.numpy as jnp
from jax import lax
from jax.experimental import pallas as pl
from jax.experimental.pallas import tpu as pltpu

F32 = jnp.float32
BF16 = jnp.bfloat16
I32 = jnp.int32

EPS = 1e-6
PAGE_SIZE = 128
GLA_HEADS = 4
GLA_RANK = 16
GLA_TAU = 16.0
GLA_CHUNK = 64
ATT_HEADS = 8
IDX_HEADS = 16
TOPK_MAX = 256

LANES = 128
SUBLANES = 8
VMEM_LIMIT_BYTES = 56 * 1024 * 1024

NEG_BIG = -1e30
INT_MIN = -2147483648
KEY_NEG_INF = -2139095041


def _cparams(sem):
    return pltpu.CompilerParams(dimension_semantics=sem, vmem_limit_bytes=VMEM_LIMIT_BYTES)


def _dot(a, b):
    return jnp.dot(a, b, preferred_element_type=F32)


def _dot_nt(a, b):
    return lax.dot_general(a, b, (((1,), (1,)), ((), ())), preferred_element_type=F32)


def _sigmoid(x):
    return 1.0 / (1.0 + jnp.exp(-x))


def _sort_key(x):
    bits = pltpu.bitcast(x + 0.0, I32)
    return jnp.where(bits >= 0, bits, bits ^ jnp.int32(0x7FFFFFFF))


def _mod_kernel(c_ref, w_ref, b_ref, o_ref):
    o_ref[0] = _dot(c_ref[...].astype(BF16), w_ref[0].astype(BF16)) + b_ref[0]


def _modulation(c_all, w_c, b_c):
    depth, d, n = w_c.shape
    rows = c_all.shape[0]
    tn = 768
    return pl.pallas_call(
        _mod_kernel,
        grid=(depth, n // tn),
        in_specs=[pl.BlockSpec((rows, d), lambda l, j: (0, 0)),
                  pl.BlockSpec((1, d, tn), lambda l, j: (l, 0, j)),
                  pl.BlockSpec((1, 1, tn), lambda l, j: (l, 0, j))],
        out_specs=pl.BlockSpec((1, rows, tn), lambda l, j: (l, 0, j)),
        out_shape=jax.ShapeDtypeStruct((depth, rows, n), F32),
        compiler_params=_cparams(("parallel", "parallel")),
        name="adaln_modulation",
    )(c_all, w_c, b_c.reshape(depth, 1, n))


def _adaln_norm(x, g, scale, shift):
    y = x * lax.rsqrt(jnp.mean(x * x, axis=-1, keepdims=True) + EPS) * g
    return y * (1.0 + scale) + shift


def _row_spec(a, tm, ngrid):
    rb = 1 if a.shape[1] == 1 else tm
    if ngrid == 2:
        return pl.BlockSpec((1, rb, a.shape[2]), (lambda b, i: (b, 0, 0)) if rb == 1 else (lambda b, i: (b, i, 0)))
    return pl.BlockSpec((1, rb, a.shape[2]), (lambda b, i, j: (b, 0, 0)) if rb == 1 else (lambda b, i, j: (b, i, 0)))


def _prenorm_kernel(x_ref, sc_ref, sh_ref, g_ref, o_ref):
    o_ref[0] = _adaln_norm(x_ref[0], g_ref[...], sc_ref[0], sh_ref[0]).astype(o_ref.dtype)


def _prenorm(x, scale, shift, g, tm):
    B, T, D = x.shape
    return pl.pallas_call(
        _prenorm_kernel,
        grid=(B, T // tm),
        in_specs=[pl.BlockSpec((1, tm, D), lambda b, i: (b, i, 0)), _row_spec(scale, tm, 2), _row_spec(shift, tm, 2),
                  pl.BlockSpec((1, D), lambda b, i: (0, 0))],
        out_specs=pl.BlockSpec((1, tm, D), lambda b, i: (b, i, 0)),
        out_shape=jax.ShapeDtypeStruct((B, T, D), BF16),
        compiler_params=_cparams(("parallel", "parallel")),
        name="prenorm",
    )(x, scale, shift, g.reshape(1, D))


def _inproj_kernel(h_ref, w_ref, o_ref):
    o_ref[0] = _dot(h_ref[0], w_ref[...]).astype(o_ref.dtype)


def _inproj(h, w_all, layer, out_dtype, tm, tn, name):
    B, T, D = h.shape
    N = w_all.shape[2]
    return pl.pallas_call(
        _inproj_kernel,
        grid=(N // tn, B, T // tm),
        in_specs=[pl.BlockSpec((1, tm, D), lambda j, b, i: (b, i, 0)),
                  pl.BlockSpec((None, D, tn), lambda j, b, i: (layer, 0, j))],
        out_specs=pl.BlockSpec((1, tm, tn), lambda j, b, i: (b, i, j)),
        out_shape=jax.ShapeDtypeStruct((B, T, N), out_dtype),
        compiler_params=_cparams(("parallel", "parallel", "parallel")),
        name=name,
    )(h, w_all)


def _inproj_kv_kernel(h_ref, w_ref, *refs, aw, n_alias):
    k32, ik32, v32, ga32, kb, ikb, vb = refs[n_alias:]
    r = _dot(h_ref[0], w_ref[...])
    t = aw + LANES
    k32[0] = r[:, :aw]
    ik32[0] = r[:, aw:t]
    v32[0] = r[:, t:t + aw]
    ga32[0] = r[:, t + aw:]
    kb[0] = r[:, :aw].astype(BF16)
    ikb[0] = r[:, aw:t].astype(BF16)
    vb[0] = r[:, t:t + aw].astype(BF16)


def _inproj_kv(h, w_all, layer, stacked, depth_out, layer_out, tm):
    B, T, D = h.shape
    aw = ATT_HEADS * LANES
    stacked = () if stacked is None else tuple(stacked)
    any_spec = pl.BlockSpec(memory_space=pl.ANY)
    lay = lambda w: pl.BlockSpec((None, 1, tm, w), lambda b, i: (layer_out, b, i, 0))
    per = lambda w: pl.BlockSpec((1, tm, w), lambda b, i: (b, i, 0))
    f32s = lambda w: jax.ShapeDtypeStruct((depth_out, B, T, w), F32)
    outs = pl.pallas_call(
        functools.partial(_inproj_kv_kernel, aw=aw, n_alias=len(stacked)),
        grid=(B, T // tm),
        in_specs=[pl.BlockSpec((1, tm, D), lambda b, i: (b, i, 0)),
                  pl.BlockSpec((None, D, w_all.shape[2]), lambda b, i: (layer, 0, 0),
                               pipeline_mode=pl.Buffered(1))] + [any_spec] * len(stacked),
        out_specs=[lay(aw), lay(LANES), lay(aw), per(LANES), per(aw), per(LANES), per(aw)],
        out_shape=[f32s(aw), f32s(LANES), f32s(aw), jax.ShapeDtypeStruct((B, T, LANES), F32),
                   jax.ShapeDtypeStruct((B, T, aw), BF16), jax.ShapeDtypeStruct((B, T, LANES), BF16),
                   jax.ShapeDtypeStruct((B, T, aw), BF16)],
        input_output_aliases={2: 0, 3: 2, 4: 1} if stacked else {},
        compiler_params=_cparams(("parallel", "parallel")),
        name="inproj_kv",
    )(h, w_all, *stacked)
    k32, ik32, v32, ga32, kb, ikb, vb = outs
    return (k32, v32, ik32), ga32, kb, ikb, vb


def _gla_constants(C):
    r = np.arange(C)[:, None]
    j = np.arange(C)[None, :]
    blocks = [(j <= r), (j > r)]
    masks = []
    w = C // 2
    while w >= 1:
        mid = (r // (2 * w)) * (2 * w) + w
        upper = r >= mid
        blocks.append(np.where(upper, (j >= mid) & (j <= r), (j > r) & (j < mid)))
        t, s = r, j
        same = (t // (2 * w)) == (s // (2 * w))
        t_up = t >= (t // (2 * w)) * (2 * w) + w
        s_lo = s < (s // (2 * w)) * (2 * w) + w
        masks.append(same & t_up & s_lo)
        w //= 2
    masks.append(r == j)
    W = np.concatenate(blocks, axis=0).astype(np.float32)
    W = np.concatenate([W, W, W], axis=1)
    M = np.stack(masks).astype(np.float32)
    return jnp.asarray(W, dtype=BF16), jnp.asarray(M, dtype=F32)


def _gla_kernel(q_ref, k_ref, v_ref, z_ref, ga_ref, wa_ref, ba_ref, gg_ref, s0_ref, W_ref, M_ref,
                o_ref, sout_ref, S_ref, *, C, nchunk, nlev, valid, H, dk, dv):
    ci = pl.program_id(1)

    @pl.when(ci == 0)
    def _():
        S_ref[...] = s0_ref[0]

    W = W_ref[...]
    for c in range(nchunk):
        rows = pl.ds(c * C, C)
        zz = _dot(ga_ref[0, rows, :].astype(BF16), wa_ref[...]) + ba_ref[...]
        g = -(jnp.maximum(-zz, 0.0) + jnp.log1p(jnp.exp(-jnp.abs(zz)))) * (1.0 / GLA_TAU)
        if valid is not None:
            tok = ci * (nchunk * C) + c * C + lax.broadcasted_iota(I32, g.shape, 0)
            g = jnp.where(tok < valid, g, 0.0)
        g1 = g.astype(BF16)
        r1 = g - g1.astype(F32)
        g2 = r1.astype(BF16)
        g3 = (r1 - g2.astype(F32)).astype(BF16)
        E_all = jnp.exp(_dot(W, jnp.concatenate([g1, g2, g3], axis=0)))
        for h in range(H):
            kc = slice(h * dk, (h + 1) * dk)
            vc = slice(h * dv, (h + 1) * dv)
            E = E_all[:, kc]
            qs = q_ref[0, rows, kc] * (dk ** -0.5)
            k = k_ref[0, rows, kc]
            v = v_ref[0, rows, vc].astype(BF16)
            e_b = E[0:C]
            e_rest = E[C:2 * C]
            A = jnp.where(M_ref[nlev] > 0, _dot_nt(qs.astype(BF16), k.astype(BF16)), 0.0)
            for l in range(nlev):
                e_l = E[(2 + l) * C:(3 + l) * C]
                A = A + jnp.where(M_ref[l] > 0, _dot_nt((qs * e_l).astype(BF16), (k * e_l).astype(BF16)), 0.0)
            S = S_ref[h]
            o = _dot(A.astype(BF16), v) + _dot((qs * e_b).astype(BF16), S.astype(BF16))
            k_dec = k * e_rest
            e_last = e_b.T[:, C - 1:C]
            S_ref[h] = e_last * S + _dot(k_dec.T.astype(BF16), v)
            y = o * lax.rsqrt(jnp.mean(o * o, axis=-1, keepdims=True) + EPS) * gg_ref[h:h + 1, :]
            zg = z_ref[0, rows, vc]
            o_ref[0, rows, vc] = (y * (zg * _sigmoid(zg))).astype(o_ref.dtype)

    @pl.when(ci == pl.num_programs(1) - 1)
    def _():
        sout_ref[0] = S_ref[...]


def _gla(p1, p2, col, s0, wa, ba, gg, consts, tc, valid):
    B, T, _ = p1.shape
    H, dk, dv = s0.shape[1], s0.shape[2], s0.shape[3]
    W, M = consts
    C = GLA_CHUNK
    nlev = M.shape[0] - 1
    qo, ko, vo, zo = (col[n] // (H * w) for n, w in (("gq", dk), ("gk", dk), ("gv", dv), ("gz", dv)))
    gao = col["gaiw"] // LANES
    kern = functools.partial(_gla_kernel, C=C, nchunk=tc // C, nlev=nlev, valid=valid, H=H, dk=dk, dv=dv)
    return pl.pallas_call(
        kern,
        grid=(B, T // tc),
        in_specs=[pl.BlockSpec((1, tc, H * dk), lambda b, i: (b, i, qo)),
                  pl.BlockSpec((1, tc, H * dk), lambda b, i: (b, i, ko)),
                  pl.BlockSpec((1, tc, H * dv), lambda b, i: (b, i, vo)),
                  pl.BlockSpec((1, tc, H * dv), lambda b, i: (b, i, zo)),
                  pl.BlockSpec((1, tc, LANES), lambda b, i: (b, i, gao)),
                  pl.BlockSpec((LANES, H * dk), lambda b, i: (0, 0)),
                  pl.BlockSpec((1, H * dk), lambda b, i: (0, 0)),
                  pl.BlockSpec((H, dv), lambda b, i: (0, 0)),
                  pl.BlockSpec((1, H, dk, dv), lambda b, i: (b, 0, 0, 0)),
                  pl.BlockSpec(W.shape, lambda b, i: (0, 0)),
                  pl.BlockSpec(M.shape, lambda b, i: (0, 0, 0))],
        out_specs=[pl.BlockSpec((1, tc, H * dv), lambda b, i: (b, i, 0)),
                   pl.BlockSpec((1, H, dk, dv), lambda b, i: (b, 0, 0, 0))],
        out_shape=[jax.ShapeDtypeStruct((B, T, H * dv), BF16),
                   jax.ShapeDtypeStruct((B, H, dk, dv), F32)],
        scratch_shapes=[pltpu.VMEM((H, dk, dv), F32)],
        compiler_params=_cparams(("parallel", "arbitrary")),
        name="gla_branch",
    )(p1, p1, p1, p1, p2, wa, ba, gg, s0, W, M)


def _kth_key(count_ge, kk, rows):
    def body(i, u):
        cand = u | (jnp.int32(1) << (31 - i))
        ok = count_ge(cand ^ jnp.int32(INT_MIN)) >= kk
        return jnp.where(ok, cand, u)
    u = lax.fori_loop(0, 32, body, jnp.zeros((rows, 1), I32))
    return u ^ jnp.int32(INT_MIN)


def _kth_key_groups(keys, kk):
    def body(i, us):
        bit = jnp.int32(1) << (31 - i)
        cands = [u | bit for u in us]
        oks = [_count(key >= (c ^ jnp.int32(INT_MIN))) >= kk for key, c in zip(keys, cands)]
        return tuple(jnp.where(ok, c, u) for ok, c, u in zip(oks, cands, us))
    us = lax.fori_loop(0, 32, body, tuple(jnp.zeros((k.shape[0], 1), I32) for k in keys), unroll=2)
    return jnp.concatenate([u ^ jnp.int32(INT_MIN) for u in us], axis=0)


def _tie_cutoff(count_eq_below, need, rows, nbits):
    def body(i, c):
        cand = c | (jnp.int32(1) << (nbits - 1 - i))
        ok = count_eq_below(cand) <= need
        return jnp.where(ok, cand, c)
    return lax.fori_loop(0, nbits, body, jnp.zeros((rows, 1), I32))


def _count(mask):
    return jnp.sum(jnp.where(mask, 1.0, 0.0), axis=1, keepdims=True)


def _dsa_block(slope_ref, iq_ref, ik_ref, iw_ref, q_ref, k_ref, v_ref, z_ref, o_ref, bias_ref, *, S, tq, topk, iw_off,
               scale, idx_scale):
    q0 = pl.program_id(1) * tq
    pos = q0 + lax.broadcasted_iota(I32, (tq, S), 0)
    kpos = lax.broadcasted_iota(I32, (tq, S), 1)
    head_cols = lambda h: pl.ds(pl.multiple_of(h * LANES, LANES), LANES)

    if S <= topk:
        bias_ref[:, 0:S] = jnp.where(kpos <= pos, 0.0, NEG_BIG)
    else:
        iw = iw_ref[0]
        w = [iw[:, iw_off + h:iw_off + h + 1] * idx_scale for h in range(IDX_HEADS)]
        kt = LANES if S % (2 * LANES) else 2 * LANES
        tiles = []
        for c in range(0, S, kt):
            ik = ik_ref[0, c:c + kt, :]
            acc = jnp.zeros((tq, kt), F32)
            for h in range(IDX_HEADS):
                acc = acc + jnp.maximum(_dot_nt(iq_ref[0, :, h * LANES:(h + 1) * LANES], ik), 0.0) * w[h]
            tiles.append(acc)
        key = _sort_key(jnp.where(kpos <= pos, jnp.concatenate(tiles, axis=1), -jnp.inf))

        kk = float(topk)
        ng = 2
        thr = _kth_key_groups([key[i * tq // ng:(i + 1) * tq // ng] for i in range(ng)], kk)
        thr_sel = jnp.maximum(thr, KEY_NEG_INF + 1)
        bias_ref[:, 0:S] = jnp.where(key >= thr_sel, 0.0, NEG_BIG)
        tied = jnp.where(_count(key >= thr_sel) > kk, 1.0, 0.0)

        @pl.when(jnp.max(tied) > 0.0)
        def _():
            need = kk - _count(key > thr)
            eq_pos = jnp.where(key == thr, kpos, jnp.int32(2 ** 30))
            cut = _tie_cutoff(lambda c_: _count(eq_pos < c_), need, tq, int(S).bit_length())
            keep_eq = jnp.where(key == thr, jnp.where(kpos < cut, 0.0, NEG_BIG), NEG_BIG)
            bias_ref[:, 0:S] = jnp.where(key >= thr_sel, jnp.where(key > thr, 0.0, keep_eq), NEG_BIG)

    log2e = 1.4426950408889634
    krel = (lax.broadcasted_iota(I32, (1, S), 1) - q0).astype(F32)

    def attend(h, carry):
        cols = head_cols(h)
        qk = _dot_nt(q_ref[0, :, cols], k_ref[0, 0:S, cols])
        logits = qk * (scale * log2e) + krel * (slope_ref[h] * log2e) + bias_ref[:, 0:S]
        m = jnp.max(logits, axis=1, keepdims=True)
        p = jnp.exp2(logits - m)
        l = jnp.sum(p, axis=1, keepdims=True)
        out = _dot(p.astype(BF16), v_ref[0, 0:S, cols]) / l
        zg = z_ref[0, :, cols]
        o_ref[0, :, cols] = (out * (zg * _sigmoid(zg))).astype(o_ref.dtype)
        return carry

    lax.fori_loop(0, ATT_HEADS, attend, 0, unroll=4)


def _key_extents(T, tq, topk):
    marks = sorted({min(T, max(tq, topk)), T // 4, 3 * T // 8, T // 2, 3 * T // 4, T})
    return [m for m in marks if m % tq == 0 and m >= tq]


def _dsa_kernel(*refs, tq, T, extents, **kw):
    last = (pl.program_id(1) + 1) * tq
    lo = 0
    for S in extents:
        pl.when((last > lo) & (last <= S))(functools.partial(_dsa_block, *refs, S=S, tq=tq, **kw))
        lo = S


def _dsa_prompt(p1, p3, ga32, ikb, kb, vb, col1, col3, topk, tq):
    B, T, _ = p1.shape
    aw = ATT_HEADS * LANES
    kern = functools.partial(_dsa_kernel, tq=tq, T=T, extents=_key_extents(T, tq, topk), topk=topk, iw_off=GLA_RANK,
                             scale=LANES ** -0.5, idx_scale=(LANES * IDX_HEADS) ** -0.5)
    iq_o = col3["iq"] // (IDX_HEADS * LANES)
    slopes = jnp.asarray(2.0 ** -(np.arange(ATT_HEADS) + 1.0), F32)
    return pl.pallas_call(
        kern,
        grid=(B, T // tq),
        in_specs=[pl.BlockSpec(memory_space=pltpu.SMEM),
                  pl.BlockSpec((1, tq, IDX_HEADS * LANES), lambda b, i: (b, i, iq_o)),
                  pl.BlockSpec((1, T, LANES), lambda b, i: (b, 0, 0)),
                  pl.BlockSpec((1, tq, LANES), lambda b, i: (b, i, 0)),
                  pl.BlockSpec((1, tq, aw), lambda b, i: (b, i, col3["aq"] // aw)),
                  pl.BlockSpec((1, T, aw), lambda b, i: (b, 0, 0)),
                  pl.BlockSpec((1, T, aw), lambda b, i: (b, 0, 0)),
                  pl.BlockSpec((1, tq, aw), lambda b, i: (b, i, col1["az"] // aw))],
        out_specs=pl.BlockSpec((1, tq, aw), lambda b, i: (b, i, 0)),
        out_shape=jax.ShapeDtypeStruct((B, T, aw), BF16),
        scratch_shapes=[pltpu.VMEM((tq, T), F32)],
        compiler_params=_cparams(("parallel", "arbitrary")),
        name="dsa_prompt",
    )(slopes, p3, ikb, ga32, p3, kb, vb, p1)


def _page_specs(n, block, layer, pages_per_step):
    def spec(g):
        return pl.BlockSpec(block, lambda b, p, pt: (layer, pt[b, p * pages_per_step + g]) + (0,) * (len(block) - 2))
    return [spec(g) for g in range(n)]


def _sample_scores_kernel(pt_ref, iq_ref, wb_ref, *rest, idx_scale):
    kidx_refs, o_ref = rest[:-1], rest[-1]
    iq = iq_ref[0]
    w = wb_ref[0] * idx_scale
    tq = o_ref.shape[1]
    for g, kidx_ref in enumerate(kidx_refs):
        s = jnp.maximum(_dot_nt(iq, kidx_ref[...].astype(BF16)), 0.0) * w
        o_ref[0, :, g * PAGE_SIZE:(g + 1) * PAGE_SIZE] = jnp.sum(s.reshape(IDX_HEADS, tq, PAGE_SIZE), axis=0)


def _sample_scores(page_table, iq_rows, wb, cache_kidx, layer, pages_per_step):
    B, n_pages = page_table.shape
    tq = iq_rows.shape[1] // IDX_HEADS
    G = pages_per_step
    kern = functools.partial(_sample_scores_kernel, idx_scale=(LANES * IDX_HEADS) ** -0.5)
    return pl.pallas_call(
        kern,
        grid_spec=pltpu.PrefetchScalarGridSpec(
            num_scalar_prefetch=1,
            grid=(B, n_pages // G),
            in_specs=[pl.BlockSpec((1, IDX_HEADS * tq, LANES), lambda b, p, pt: (b, 0, 0)),
                      pl.BlockSpec((1, IDX_HEADS * tq, LANES), lambda b, p, pt: (b, 0, 0))]
                     + _page_specs(G, (None, None, PAGE_SIZE, LANES), layer, G),
            out_specs=pl.BlockSpec((1, tq, G * PAGE_SIZE), lambda b, p, pt: (b, 0, p)),
        ),
        out_shape=jax.ShapeDtypeStruct((B, tq, n_pages * PAGE_SIZE), F32),
        compiler_params=_cparams(("parallel", "arbitrary")),
        name="sample_scores",
    )(page_table, iq_rows, wb, *([cache_kidx] * G))


def _sample_attn_kernel(pt_ref, sc_ref, iq_ref, wb_ref, ikn_ref, q_ref, kn_ref, vn_ref, z_ref, *rest,
                        G, tq, past, topk, scale, idx_scale):
    ck_refs, cv_refs, o_ref = rest[:G], rest[G:2 * G], rest[2 * G]
    qs_ref, thr_ref, cut_ref, bnew_ref, m_ref, l_ref, acc_ref = rest[2 * G + 1:]
    p = pl.program_id(1)
    H = ATT_HEADS
    lane = lax.broadcasted_iota(I32, (tq, LANES), 1)
    row = lax.broadcasted_iota(I32, (tq, LANES), 0)

    @pl.when(p == 0)
    def _():
        q = q_ref[0].astype(F32)
        qs_ref[...] = jnp.concatenate([q[:, h * LANES:(h + 1) * LANES] for h in range(H)], axis=0)
        ikn = jnp.concatenate([ikn_ref[0], jnp.zeros((LANES - tq, LANES), F32)], axis=0).astype(BF16)
        s = jnp.maximum(_dot_nt(iq_ref[0], ikn), 0.0) * (wb_ref[0] * idx_scale)
        s_new = jnp.sum(s.reshape(IDX_HEADS, tq, LANES), axis=0)
        key_new = _sort_key(jnp.where((lane <= row) & (lane < tq), s_new, -jnp.inf))
        key_past = _sort_key(sc_ref[0])

        kk = float(topk)
        count_ge = lambda t: _count(key_past >= t) + _count(key_new >= t)
        thr = _kth_key(count_ge, kk, tq)
        thr_ref[...] = jnp.broadcast_to(thr, thr_ref.shape)
        cut_ref[...] = jnp.full(cut_ref.shape, past + LANES, I32)
        tied = jnp.where((count_ge(thr) > kk) & (thr > KEY_NEG_INF), 1.0, 0.0)

        @pl.when(jnp.max(tied) > 0.0)
        def _():
            need = kk - _count(key_past > thr) - _count(key_new > thr)
            idx_past = lax.broadcasted_iota(I32, key_past.shape, 1)
            eq_past = key_past == thr
            eq_new = key_new == thr
            cnt = lambda c_: _count(eq_past & (idx_past < c_)) + _count(eq_new & (lane + past < c_))
            c = _tie_cutoff(cnt, need, tq, int(past + LANES).bit_length())
            cut_ref[...] = jnp.broadcast_to(c, cut_ref.shape)

        sel_new = (key_new > thr) | ((key_new == thr) & (lane + past < cut_ref[...]))
        bnew_ref[...] = jnp.where(sel_new & (lane <= row) & (lane < tq), 0.0, NEG_BIG)
        m_ref[...] = jnp.full(m_ref.shape, -jnp.inf, F32)
        l_ref[...] = jnp.zeros(l_ref.shape, F32)
        acc_ref[...] = jnp.zeros(acc_ref.shape, F32)

    thr = thr_ref[...]
    cut = cut_ref[...]
    bias, dist = [], []
    for g in range(G):
        first = pl.multiple_of((p * G + g) * PAGE_SIZE, PAGE_SIZE)
        key_p = _sort_key(sc_ref[0, :, pl.ds(first, PAGE_SIZE)])
        kpos = lane + first
        keep_eq = jnp.where(key_p == thr, jnp.where(kpos < cut, 0.0, NEG_BIG), NEG_BIG)
        bias.append(jnp.where(key_p > thr, 0.0, keep_eq))
        dist.append((row + past - kpos).astype(F32))
    head_rows = [pl.ds(h, PAGE_SIZE, stride=H) for h in range(H)]
    qs = qs_ref[...]
    logits = []
    for h in range(H):
        q_h = qs[h * tq:(h + 1) * tq].astype(BF16)
        logits.append([_dot_nt(q_h, ck_refs[g][head_rows[h], :].astype(BF16)) * scale
                       - (2.0 ** -(h + 1)) * dist[g] + bias[g] for g in range(G)])
    m_old = m_ref[:, 0:1]
    page_max = jnp.concatenate([functools.reduce(jnp.maximum, lg) for lg in logits], axis=0)
    m_new = jnp.maximum(m_old, jnp.max(page_max, axis=1, keepdims=True))
    alpha = jnp.exp(m_old - m_new)
    pe = [[jnp.exp(logits[h][g] - m_new[h * tq:(h + 1) * tq]) for g in range(G)] for h in range(H)]
    psum = jnp.concatenate([functools.reduce(jnp.add, ph) for ph in pe], axis=0)
    pv = [functools.reduce(jnp.add, [_dot(pe[h][g].astype(BF16), cv_refs[g][head_rows[h], :].astype(BF16))
                                     for g in range(G)]) for h in range(H)]
    l_new = alpha * l_ref[:, 0:1] + jnp.sum(psum, axis=1, keepdims=True)
    acc_ref[...] = alpha * acc_ref[...] + jnp.concatenate(pv, axis=0)
    l_ref[...] = jnp.broadcast_to(l_new, l_ref.shape)
    m_ref[...] = jnp.broadcast_to(m_new, m_ref.shape)

    @pl.when(p == pl.num_programs(1) - 1)
    def _():
        zpad = jnp.zeros((LANES - tq, LANES), F32)
        bnew = bnew_ref[...]
        dist = (row - lane).astype(F32)
        qs = qs_ref[...].astype(BF16)
        lg = []
        for h in range(H):
            cols = slice(h * LANES, (h + 1) * LANES)
            kn = jnp.concatenate([kn_ref[0, :, cols], zpad], axis=0).astype(BF16)
            qk = _dot_nt(qs, kn)[h * tq:(h + 1) * tq]
            lg.append(qk * scale - (2.0 ** -(h + 1)) * dist + bnew)
        logits_new = jnp.concatenate(lg, axis=0)
        m_old = m_ref[:, 0:1]
        m_new = jnp.maximum(m_old, jnp.max(logits_new, axis=1, keepdims=True))
        alpha = jnp.exp(m_old - m_new)
        pe = jnp.exp(logits_new - m_new)
        l_fin = alpha * l_ref[:, 0:1] + jnp.sum(pe, axis=1, keepdims=True)
        acc = alpha * acc_ref[...]
        pe = pe.astype(BF16)
        for h in range(H):
            cols = slice(h * LANES, (h + 1) * LANES)
            vn = jnp.concatenate([vn_ref[0, :, cols], zpad], axis=0).astype(BF16)
            rows = slice(h * tq, (h + 1) * tq)
            out = (acc[rows] + _dot(pe, vn)[rows]) / l_fin[rows]
            zg = z_ref[0, :, cols]
            o_ref[0, :, cols] = (out * (zg * _sigmoid(zg))).astype(o_ref.dtype)


def _sample_attn(page_table, scores, iq_rows, wb, ik_new, q, k_new, v_new, z, cache_k, cache_v, layer, topk,
                 pages_per_step):
    B, n_pages = page_table.shape
    tq = q.shape[1]
    H = ATT_HEADS
    aw = H * LANES
    past = n_pages * PAGE_SIZE
    G = pages_per_step
    kern = functools.partial(_sample_attn_kernel, G=G, tq=tq, past=past, topk=topk, scale=LANES ** -0.5,
                             idx_scale=(LANES * IDX_HEADS) ** -0.5)
    per_b = lambda b, p, pt: (b, 0, 0)
    page_block = (None, None, PAGE_SIZE * H, LANES)
    cache_k, cache_v = (c.reshape(c.shape[0], c.shape[1], PAGE_SIZE * H, LANES) for c in (cache_k, cache_v))
    return pl.pallas_call(
        kern,
        grid_spec=pltpu.PrefetchScalarGridSpec(
            num_scalar_prefetch=1,
            grid=(B, n_pages // G),
            in_specs=[pl.BlockSpec((1, tq, past), per_b),
                      pl.BlockSpec((1, IDX_HEADS * tq, LANES), per_b),
                      pl.BlockSpec((1, IDX_HEADS * tq, LANES), per_b),
                      pl.BlockSpec((1, tq, LANES), per_b),
                      pl.BlockSpec((1, tq, aw), per_b),
                      pl.BlockSpec((1, tq, aw), per_b),
                      pl.BlockSpec((1, tq, aw), per_b),
                      pl.BlockSpec((1, tq, aw), per_b)]
                     + _page_specs(G, page_block, layer, G) + _page_specs(G, page_block, layer, G),
            out_specs=pl.BlockSpec((1, tq, aw), per_b),
            scratch_shapes=[pltpu.VMEM((H * tq, LANES), F32),
                            pltpu.VMEM((tq, LANES), I32),
                            pltpu.VMEM((tq, LANES), I32),
                            pltpu.VMEM((tq, LANES), F32),
                            pltpu.VMEM((H * tq, LANES), F32),
                            pltpu.VMEM((H * tq, LANES), F32),
                            pltpu.VMEM((H * tq, LANES), F32)],
        ),
        out_shape=jax.ShapeDtypeStruct((B, tq, aw), BF16),
        compiler_params=_cparams(("parallel", "arbitrary")),
        name="sample_attention",
    )(page_table, scores, iq_rows, wb, ik_new, q, k_new, v_new, z, *([cache_k] * G), *([cache_v] * G))


def _outproj_kernel(oa_ref, ob_ref, ma_ref, mb_ref, x_ref, gate_ref, wpa_ref, wpb_ref, wo_ref, g_ref, *rest, final):
    pa = _dot(oa_ref[0], wpa_ref[...])
    pb = _dot(ob_ref[0], wpb_ref[...])
    merged = _sigmoid(ma_ref[0]) * pa + _sigmoid(mb_ref[0]) * pb
    x = x_ref[0] + gate_ref[0] * _dot(merged.astype(BF16), wo_ref[...])
    if final:
        (o_ref,) = rest
        o_ref[0] = x * lax.rsqrt(jnp.mean(x * x, axis=-1, keepdims=True) + EPS) * g_ref[...]
    else:
        sc_ref, sh_ref, o_ref, h_ref = rest
        o_ref[0] = x
        h_ref[0] = _adaln_norm(x, g_ref[...], sc_ref[0], sh_ref[0]).astype(h_ref.dtype)


def _outproj(o_a, o_b, p1, col1, x, gate, w_pa, w_pb, w_out, layer, g, nxt, tm):
    B, T, D = x.shape
    final = nxt is None
    const = lambda b, i: (0, 0)
    wspec = lambda w: pl.BlockSpec((None,) + w.shape[1:], lambda b, i: (layer, 0, 0), pipeline_mode=pl.Buffered(1))
    xspec = pl.BlockSpec((1, tm, D), lambda b, i: (b, i, 0))
    extra = [] if final else [_row_spec(nxt[0], tm, 2), _row_spec(nxt[1], tm, 2)]
    return pl.pallas_call(
        functools.partial(_outproj_kernel, final=final),
        grid=(B, T // tm),
        in_specs=[pl.BlockSpec((1, tm, o_a.shape[2]), lambda b, i: (b, i, 0)),
                  pl.BlockSpec((1, tm, o_b.shape[2]), lambda b, i: (b, i, 0)),
                  pl.BlockSpec((1, tm, D), lambda b, i: (b, i, col1["m_a"] // D)),
                  pl.BlockSpec((1, tm, D), lambda b, i: (b, i, col1["m_b"] // D)),
                  xspec, _row_spec(gate, tm, 2), wspec(w_pa), wspec(w_pb), wspec(w_out),
                  pl.BlockSpec((1, D), const)] + extra,
        out_specs=xspec if final else [xspec, xspec],
        out_shape=(jax.ShapeDtypeStruct((B, T, D), F32) if final else
                   [jax.ShapeDtypeStruct((B, T, D), F32), jax.ShapeDtypeStruct((B, T, D), BF16)]),
        compiler_params=_cparams(("parallel", "parallel")),
        name="merge_outproj",
    )(o_a, o_b, p1, p1, x, gate, w_pa, w_pb, w_out, g.reshape(1, D), *([] if final else nxt))


def _column_groups(d):
    gqk, gv, att, iqw = d // 4, d // 2, d // 2, d
    sizes = [("gq", gqk), ("gk", gqk), ("gv", gv), ("gz", gv), ("ga", GLA_RANK), ("aq", att), ("ak", att),
             ("av", att), ("az", att), ("iq", iqw), ("ik", LANES), ("iw", IDX_HEADS), ("m_a", d), ("m_b", d)]
    src, o = {}, 0
    for n, s in sizes:
        src[n] = (o, o + s)
        o += s
    groups = (("gq", "gk", "gv", "gz", "az", "m_a", "m_b"), ("ak", "ik", "av", "gaiw"), ("iq", "aq"))
    return src, groups


def _group_layout(src, names):
    offs, o = {}, 0
    for n in names:
        offs[n] = o
        o += LANES if n == "gaiw" else src[n][1] - src[n][0]
    return offs, o


def _regroup_kernel(wt_ref, *out_refs, src, groups):
    rows = 4 * LANES
    for names, o_ref in zip(groups, out_refs):
        offs, _ = _group_layout(src, names)
        for n in names:
            if n == "gaiw":
                pad = jnp.zeros((LANES - GLA_RANK - IDX_HEADS, LANES), F32)
                blk = jnp.concatenate([wt_ref[src["ga"][0]:src["ga"][1], :], wt_ref[src["iw"][0]:src["iw"][1], :],
                                       pad], axis=0)
                o_ref[:, offs[n]:offs[n] + LANES] = blk.T.astype(BF16)
            else:
                a, b = src[n]
                for r in range(a, b, rows):
                    e = min(b, r + rows)
                    o_ref[:, offs[n] + r - a:offs[n] + e - a] = wt_ref[r:e, :].T.astype(BF16)


def _regroup_weights(w, src, groups):
    depth, D, n_in = w.shape
    widths = [_group_layout(src, names)[1] for names in groups]
    return pl.pallas_call(
        functools.partial(_regroup_kernel, src=src, groups=groups),
        grid=(depth, D // LANES),
        in_specs=[pl.BlockSpec((None, n_in, LANES), lambda l, i: (l, 0, i))],
        out_specs=[pl.BlockSpec((None, LANES, n), lambda l, i: (l, i, 0)) for n in widths],
        out_shape=[jax.ShapeDtypeStruct((depth, D, n), BF16) for n in widths],
        compiler_params=_cparams(("parallel", "parallel")),
        name="regroup_weights",
    )(jnp.swapaxes(w, 1, 2))


def kernel(x_prompt, x_sample, cache_k, cache_v, cache_kidx, state_gla, page_table, c_prompt, c_sample, w_c, b_c,
           g_norm, w_in, w_a2, b_a2, g_gla, w_pa, w_pb, w_out, g_final):
    depth = w_in.shape[0]
    B, T, D = x_prompt.shape
    Bs, Ts, _ = x_sample.shape
    H, dk, dv = state_gla.shape[2], state_gla.shape[3], state_gla.shape[4]
    n_pages = page_table.shape[1]
    past = n_pages * PAGE_SIZE
    aw = ATT_HEADS * LANES
    n_s = Bs * Ts
    src, groups = _column_groups(D)

    rows = -(-(B + Bs) // SUBLANES) * SUBLANES
    c_all = jnp.concatenate([c_prompt, c_sample, jnp.zeros((rows - B - Bs, D), F32)], axis=0)
    mod = _modulation(c_all, w_c, b_c)

    w1, w2, w3 = _regroup_weights(w_in, src, groups)
    col1, col3 = _group_layout(src, groups[0])[0], _group_layout(src, groups[2])[0]
    wa = jnp.concatenate([w_a2, jnp.zeros((depth, LANES - GLA_RANK, w_a2.shape[2]), F32)], axis=1).astype(BF16)
    wpa, wpb, wo = w_pa.astype(BF16), w_pb.astype(BF16), w_out.astype(BF16)
    gla_col = {**col1, "gaiw": 0}

    gla_consts = _gla_constants(GLA_CHUNK)
    score_pages = min(32, n_pages)
    attn_pages = min(16, n_pages)
    topk_p = min(TOPK_MAX, T // 4)
    topk_s = min(TOPK_MAX, (past + Ts) // 4)
    ts_pad = GLA_CHUNK
    tm = min(T, 512)

    def modulation(l):
        shift, scale, gate = (mod[l, :, i * D:(i + 1) * D] for i in range(3))
        prompt = tuple(a[:B].reshape(B, 1, D) for a in (scale, shift, gate))
        sample = tuple(jnp.repeat(a[B:B + Bs], Ts, axis=0).reshape(1, n_s, D) for a in (scale, shift, gate))
        return prompt, sample

    xp, xs = x_prompt, x_sample.reshape(1, n_s, D)
    mod_p, mod_s = modulation(0)
    hp = _prenorm(xp, mod_p[0], mod_p[1], g_norm[0], tm)
    hs = _prenorm(xs, mod_s[0], mod_s[1], g_norm[0], n_s)
    stacked_p = None
    outs = {n: [] for n in ("sp", "ks", "vs", "iks", "ss")}
    for l in range(depth):
        final = l == depth - 1
        nxt_p, nxt_s = (None, None) if final else modulation(l + 1)
        g_next = g_final if final else g_norm[l + 1]
        ba = b_a2[l].reshape(1, -1)

        p1 = _inproj(hp, w1, l, F32, min(T, 2 * tm), 2048, "inproj_gates")
        p3 = _inproj(hp, w3, l, BF16, tm, 1024, "inproj_queries")
        stacked_p, ga32, kb, ikb, vb = _inproj_kv(hp, w2, l, stacked_p, depth, l, tm)
        s0 = jnp.zeros((B, H, dk, dv), F32)
        o_a, s_new = _gla(p1, ga32, gla_col, s0, wa[l], ba, g_gla[l], gla_consts, min(T, 256), None)
        o_b = _dsa_prompt(p1, p3, ga32, ikb, kb, vb, col1, col3, topk_p, 128)
        res = _outproj(o_a, o_b, p1, col1, xp, mod_p[2], wpa, wpb, wo, l, g_next,
                       None if final else nxt_p[:2], 256)
        xp, hp = (res, None) if final else res
        outs["sp"].append(s_new)

        q1 = _inproj(hs, w1, l, F32, n_s, 2048, "inproj_gates_s")
        q3 = _inproj(hs, w3, l, BF16, n_s, 1024, "inproj_queries_s")
        (k_s, v_s, ik_s), ga_s, _, _, _ = _inproj_kv(hs, w2, l, None, 1, 0, n_s)
        pad_t = lambda a: jnp.pad(a.reshape(Bs, Ts, -1), ((0, 0), (0, ts_pad - Ts), (0, 0)))
        o_a_s, s_new_s = _gla(pad_t(q1[0, :, :col1["az"]]), pad_t(ga_s[0]), gla_col, state_gla[l], wa[l], ba,
                              g_gla[l], gla_consts, ts_pad, Ts)
        o_a_s = o_a_s[:, :Ts].reshape(1, n_s, -1)

        iq = q3[0, :, col3["iq"]:col3["iq"] + IDX_HEADS * LANES].reshape(Bs, Ts, IDX_HEADS, LANES)
        iq_rows = jnp.transpose(iq, (0, 2, 1, 3)).reshape(Bs, IDX_HEADS * Ts, LANES)
        iw = ga_s[0, :, GLA_RANK:GLA_RANK + IDX_HEADS].reshape(Bs, Ts, IDX_HEADS)
        wb = jnp.broadcast_to(jnp.transpose(iw, (0, 2, 1)).reshape(Bs, IDX_HEADS * Ts, 1), (Bs, IDX_HEADS * Ts, LANES))
        ik_new = ik_s.reshape(Bs, Ts, LANES)
        k_new = k_s.reshape(Bs, Ts, aw)
        v_new = v_s.reshape(Bs, Ts, aw)
        aq = q3[0, :, col3["aq"]:col3["aq"] + aw].reshape(Bs, Ts, aw)
        az = q1[0, :, col1["az"]:col1["az"] + aw].reshape(Bs, Ts, aw)
        scores = _sample_scores(page_table, iq_rows, wb, cache_kidx, l, score_pages)
        o_b_s = _sample_attn(page_table, scores, iq_rows, wb, ik_new, aq, k_new, v_new, az, cache_k, cache_v,
                             l, topk_s, attn_pages)
        res = _outproj(o_a_s, o_b_s.reshape(1, n_s, aw), q1, col1, xs, mod_s[2], wpa, wpb, wo, l, g_next,
                       None if final else nxt_s[:2], n_s)
        xs, hs = (res, None) if final else res
        outs["ks"].append(k_new.reshape(Bs, Ts, ATT_HEADS, LANES))
        outs["vs"].append(v_new.reshape(Bs, Ts, ATT_HEADS, LANES))
        outs["iks"].append(ik_new)
        outs["ss"].append(s_new_s)
        mod_p, mod_s = nxt_p, nxt_s

    k_p, v_p, ik_p = stacked_p
    st = lambda n: jnp.stack(outs[n])
    return (xp, xs.reshape(Bs, Ts, D), k_p.reshape(depth, B, T, ATT_HEADS, LANES),
            v_p.reshape(depth, B, T, ATT_HEADS, LANES), ik_p, st("sp"), st("ks"), st("vs"), st("iks"), st("ss"))
```
